```python
import functools
import jax, jax.numpy as jnp
from jax import lax
import numpy as np

D_MODEL = 1024
BATCH = 4
SEQ = 8192
DEPTH = 2
DEC_BATCH = 32
DEC_SEQ = 1
PAST_LEN = 16384
PAGE_SIZE = 128

CONV_DIM = D_MODEL // 2
CONV_WIDTH = 31
N_HEADS = 8
HEAD_DIM = 64
ATTN_DIM = N_HEADS * HEAD_DIM
Q_BLOCK = 128
RMS_EPS = 1e-6
LN_EPS = 1e-5
FORGET_BIAS = 3.0
ATTN_SCALE = HEAD_DIM ** -0.5

SPLIT_SIZES = (CONV_DIM, CONV_DIM, CONV_DIM, ATTN_DIM, ATTN_DIM, ATTN_DIM, ATTN_DIM, N_HEADS, D_MODEL, D_MODEL)
IN_DIM = sum(SPLIT_SIZES)
SPLIT_IDX = tuple(int(i) for i in np.cumsum(SPLIT_SIZES)[:-1])

kernel_name = 'conformer_fox_parallel_decoder_step'


def rmsnorm(x, g):
    xf = x.astype(jnp.float32)
    y = xf * lax.rsqrt(jnp.mean(xf * xf, axis=-1, keepdims=True) + RMS_EPS)
    return (y * g.astype(jnp.float32)).astype(x.dtype)


def layernorm(x, g, b):
    xf = x.astype(jnp.float32)
    mu = jnp.mean(xf, axis=-1, keepdims=True)
    var = jnp.mean(jnp.square(xf - mu), axis=-1, keepdims=True)
    y = (xf - mu) * lax.rsqrt(var + LN_EPS)
    return (y * g.astype(jnp.float32) + b.astype(jnp.float32)).astype(x.dtype)


def causal_dwconv(u, prev, w_dw, b_dw):
    full = jnp.concatenate([prev.astype(u.dtype), u], axis=1)
    out = lax.conv_general_dilated(
        full, w_dw[:, None, :].astype(u.dtype), window_strides=(1,), padding='VALID',
        dimension_numbers=('NWC', 'WIO', 'NWC'), feature_group_count=CONV_DIM)
    return out + b_dw, full[:, -(CONV_WIDTH - 1):]


def fox_attend(q, cq, qpos, k, v, cum, kpos):
    s = jnp.einsum('bthd,bkhd->bhtk', q, k).astype(jnp.float32) * ATTN_SCALE
    s = s + jnp.transpose(cq, (0, 2, 1))[..., None] - jnp.transpose(cum, (0, 2, 1))[:, :, None, :]
    mask = kpos[None, :] <= qpos[:, None]
    s = jnp.where(mask, s, -jnp.inf)
    p = jax.nn.softmax(s, axis=-1)
    return jnp.einsum('bhtk,bkhd->bthd', p.astype(v.dtype), v)


def prompt_attend(q, k, v, logf):
    b, s_len = q.shape[0], q.shape[1]
    nb = s_len // Q_BLOCK
    cum = jnp.cumsum(logf, axis=1)
    qb = q.reshape(b, nb, Q_BLOCK, N_HEADS, HEAD_DIM).swapaxes(0, 1)
    cqb = cum.reshape(b, nb, Q_BLOCK, N_HEADS).swapaxes(0, 1)
    starts = jnp.arange(nb, dtype=jnp.int32) * Q_BLOCK
    kpos = jnp.arange(s_len, dtype=jnp.int32)
    qoff = jnp.arange(Q_BLOCK, dtype=jnp.int32)

    def block(args):
        qi, cqi, st = args
        return fox_attend(qi, cqi, st + qoff, k, v, cum, kpos)

    o = lax.map(block, (qb, cqb, starts))
    return o.swapaxes(0, 1).reshape(b, s_len, N_HEADS, HEAD_DIM)


def sample_attend(q, k, v, logf, k_pool, v_pool, lf_pool, page_table):
    db, t = q.shape[0], q.shape[1]
    past = page_table.shape[1] * PAGE_SIZE
    k_past = k_pool[page_table].reshape(db, past, N_HEADS, HEAD_DIM)
    v_past = v_pool[page_table].reshape(db, past, N_HEADS, HEAD_DIM)
    lf_past = lf_pool[page_table].reshape(db, past, N_HEADS).astype(jnp.float32)
    k_all = jnp.concatenate([k_past, k.astype(k_past.dtype)], axis=1)
    v_all = jnp.concatenate([v_past, v.astype(v_past.dtype)], axis=1)
    cum = jnp.cumsum(jnp.concatenate([lf_past, logf], axis=1), axis=1)
    qpos = past + jnp.arange(t, dtype=jnp.int32)
    kpos = jnp.arange(past + t, dtype=jnp.int32)
    return fox_attend(q.astype(k_all.dtype), cum[:, past:], qpos, k_all, v_all, cum, kpos)


def mixer_layer(x, conv_prev, attend, norm_g, w_in, b_f, w_dw, b_dw, ln_g, ln_b,
                w_conv_out, w_attn_out, w_o):
    bx, t, _ = x.shape
    h = rmsnorm(x, norm_g)
    (glu_a, glu_b, gate_c, q, k, v, gate_a, fgt, mix_c, mix_a) = jnp.split(h @ w_in, SPLIT_IDX, axis=-1)
    u = glu_a * jax.nn.sigmoid(glu_b)
    c, conv_state = causal_dwconv(u, conv_prev, w_dw, b_dw)
    c = jax.nn.silu(layernorm(c, ln_g, ln_b)) * jax.nn.silu(gate_c)
    y_c = c @ w_conv_out
    q = q.reshape(bx, t, N_HEADS, HEAD_DIM)
    k = k.reshape(bx, t, N_HEADS, HEAD_DIM)
    v = v.reshape(bx, t, N_HEADS, HEAD_DIM)
    logf = jax.nn.log_sigmoid(fgt.astype(jnp.float32) + b_f.astype(jnp.float32))
    o = attend(q, k, v, logf).reshape(bx, t, ATTN_DIM).astype(x.dtype)
    y_a = (o * jax.nn.silu(gate_a)) @ w_attn_out
    m = jax.nn.sigmoid(mix_c) * y_c + jax.nn.sigmoid(mix_a) * y_a
    return x + m @ w_o, conv_state, k, v, logf


def setup_inputs(seed: int = 0) -> dict:
    key = jax.random.key(seed)
    ks = jax.random.split(key, 20)
    n_pages = PAST_LEN // PAGE_SIZE
    n_pool = (DEC_BATCH * n_pages * 5) // 4
    f32 = jnp.float32
    x_prompt = jax.random.normal(ks[0], (BATCH, SEQ, D_MODEL), f32)
    x_sample = jax.random.normal(ks[1], (DEC_BATCH, DEC_SEQ, D_MODEL), f32)
    cache_k = jax.random.normal(ks[2], (DEPTH, n_pool, PAGE_SIZE, N_HEADS, HEAD_DIM), f32)
    cache_v = jax.random.normal(ks[3], (DEPTH, n_pool, PAGE_SIZE, N_HEADS, HEAD_DIM), f32)
    cache_logf = jax.nn.log_sigmoid(FORGET_BIAS + jax.random.normal(ks[4], (DEPTH, n_pool, PAGE_SIZE, N_HEADS), f32))
    state_conv = 0.5 * jax.random.normal(ks[5], (DEPTH, DEC_BATCH, CONV_WIDTH - 1, CONV_DIM), f32)
    page_table = jax.random.permutation(ks[6], n_pool)[:DEC_BATCH * n_pages].reshape(DEC_BATCH, n_pages).astype(jnp.int32)
    norm_g = 1.0 + 0.02 * jax.random.normal(ks[7], (DEPTH, D_MODEL), f32)
    w_in = jax.random.normal(ks[8], (DEPTH, D_MODEL, IN_DIM), f32) * D_MODEL ** -0.5
    b_f = FORGET_BIAS + 0.1 * jax.random.normal(ks[9], (DEPTH, N_HEADS), f32)
    w_dw = jax.random.normal(ks[10], (DEPTH, CONV_WIDTH, CONV_DIM), f32) * CONV_WIDTH ** -0.5
    b_dw = 0.02 * jax.random.normal(ks[11], (DEPTH, CONV_DIM), f32)
    ln_g = 1.0 + 0.02 * jax.random.normal(ks[12], (DEPTH, CONV_DIM), f32)
    ln_b = 0.02 * jax.random.normal(ks[13], (DEPTH, CONV_DIM), f32)
    w_conv_out = jax.random.normal(ks[14], (DEPTH, CONV_DIM, D_MODEL), f32) * CONV_DIM ** -0.5
    w_attn_out = jax.random.normal(ks[15], (DEPTH, ATTN_DIM, D_MODEL), f32) * ATTN_DIM ** -0.5
    w_o = jax.random.normal(ks[16], (DEPTH, D_MODEL, D_MODEL), f32) * D_MODEL ** -0.5
    final_g = 1.0 + 0.02 * jax.random.normal(ks[17], (D_MODEL,), f32)
    return {'x_prompt': x_prompt, 'x_sample': x_sample, 'cache_k': cache_k, 'cache_v': cache_v,
            'cache_logf': cache_logf, 'state_conv': state_conv, 'page_table': page_table,
            'norm_g': norm_g, 'w_in': w_in, 'b_f': b_f, 'w_dw': w_dw, 'b_dw': b_dw,
            'ln_g': ln_g, 'ln_b': ln_b, 'w_conv_out': w_conv_out, 'w_attn_out': w_attn_out,
            'w_o': w_o, 'final_g': final_g}


def reference(x_prompt, x_sample, cache_k, cache_v, cache_logf, state_conv, page_table,
              norm_g, w_in, b_f, w_dw, b_dw, ln_g, ln_b, w_conv_out, w_attn_out, w_o, final_g):
    xp, xs = x_prompt, x_sample
    kp, vp, lp, cp = [], [], [], []
    ksl, vsl, lsl, csl = [], [], [], []
    for l in range(DEPTH):
        lw = (norm_g[l], w_in[l], b_f[l], w_dw[l], b_dw[l], ln_g[l], ln_b[l],
              w_conv_out[l], w_attn_out[l], w_o[l])
        zero_prev = jnp.zeros((xp.shape[0], CONV_WIDTH - 1, CONV_DIM), xp.dtype)
        xp, c_st, k_new, v_new, lf_new = mixer_layer(xp, zero_prev, prompt_attend, *lw)
        kp.append(k_new); vp.append(v_new); lp.append(lf_new); cp.append(c_st)
        att = functools.partial(sample_attend, k_pool=cache_k[l], v_pool=cache_v[l],
                                lf_pool=cache_logf[l], page_table=page_table)
        xs, c_st, k_new, v_new, lf_new = mixer_layer(xs, state_conv[l], att, *lw)
        ksl.append(k_new); vsl.append(v_new); lsl.append(lf_new); csl.append(c_st)
    y_prompt = rmsnorm(xp, final_g)
    y_sample = rmsnorm(xs, final_g)
    return (y_prompt, y_sample,
            jnp.stack(kp), jnp.stack(vp), jnp.stack(lp), jnp.stack(cp),
            jnp.stack(ksl), jnp.stack(vsl), jnp.stack(lsl), jnp.stack(csl))
```

```python
import functools

import jax
import jax.numpy as jnp
from jax import lax
from jax.experimental import pallas as pl
from jax.experimental.pallas import tpu as pltpu

CONV_WIDTH = 31
N_HEADS = 8
HEAD_DIM = 64
PAGE_SIZE = 128
RMS_EPS = 1e-6
LN_EPS = 1e-5
ATTN_SCALE = HEAD_DIM ** -0.5

LANES = 128
SUBLANES = 8
BF16_ROWS = 16
HEADS_PER_BLOCK = LANES // HEAD_DIM
VMEM_LIMIT = 56 * 1024 * 1024

HIST = 32
CONV_ROWS = 64
NEG = -1e30

BF16 = jnp.bfloat16
F32 = jnp.float32


def _sigmoid(x):
    return 1.0 / (1.0 + jnp.exp(-x))


def _silu(x):
    return x * _sigmoid(x)


def _log_sigmoid(x):
    return jnp.minimum(x, 0.0) - jnp.log1p(jnp.exp(-jnp.abs(x)))


def _rmsnorm(x, g):
    return x * lax.rsqrt(jnp.mean(x * x, axis=-1, keepdims=True) + RMS_EPS) * g


def _layernorm(x, g, b):
    mu = jnp.mean(x, axis=-1, keepdims=True)
    xc = x - mu
    var = jnp.mean(xc * xc, axis=-1, keepdims=True)
    return xc * lax.rsqrt(var + LN_EPS) * g + b


def _dot(a, b):
    return jnp.dot(a, b, preferred_element_type=F32)


def _dot_nt(a, b):
    return lax.dot_general(a, b, (((1,), (1,)), ((), ())), preferred_element_type=F32)


def _split3(x):
    hi = x.astype(BF16)
    r = x - hi.astype(F32)
    mid = r.astype(BF16)
    lo = (r - mid.astype(F32)).astype(BF16)
    return hi, mid, lo


def _const_spec(shape, layer):
    nd = len(shape) - 1
    return pl.BlockSpec((None,) + tuple(shape[1:]), lambda *_: (layer,) + (0,) * nd)


def _proj_conv_kernel(x_ref, g_ref, wmain_ref, wkvt_ref, wf_ref, bf_ref, wft_ref, bft_ref, wdw_ref, bdw_ref,
                      lng_ref, lnb_ref, wco_ref,
                      kt32_ref, vt32_ref, lft_ref, qb_ref, ktb_ref, vtb_ref, ga_ref, cum_ref, cumt_ref, yc_ref,
                      cst_ref, h_scr, ubuf, c_scr, wb_scr, tri_scr, carry_scr, carryt_scr, *, tq, cdim, adim):
    b = pl.program_id(0)
    t = pl.program_id(1)
    nt = pl.num_programs(1)

    @pl.when((b == 0) & (t == 0))
    def _():
        row = lax.broadcasted_iota(jnp.int32, (tq, tq), 0)
        col = lax.broadcasted_iota(jnp.int32, (tq, tq), 1)
        tri_scr[...] = jnp.where(col <= row, 1.0, 0.0).astype(BF16)
        for j in range(CONV_WIDTH):
            wb_scr[j * SUBLANES:(j + 1) * SUBLANES, :] = jnp.broadcast_to(wdw_ref[j:j + 1, :], (SUBLANES, cdim))

    @pl.when(t == 0)
    def _():
        ubuf[0:HIST, :] = jnp.zeros((HIST, cdim), F32)
        carry_scr[...] = jnp.zeros_like(carry_scr)
        carryt_scr[...] = jnp.zeros_like(carryt_scr)

    x = x_ref[0]
    h_scr[...] = _rmsnorm(x, g_ref[...]).astype(BF16)

    def proj(off, width):
        return _dot(h_scr[...], wmain_ref[:, off:off + width])

    o_q = 3 * cdim
    u = proj(0, cdim) * _sigmoid(proj(cdim, cdim))
    ubuf[HIST:HIST + tq, :] = u

    shift0 = HIST - (CONV_WIDTH - 1)
    groups = CONV_ROWS // SUBLANES
    sub = lax.broadcasted_iota(jnp.int32, (SUBLANES, LANES), 0)

    def conv_step(i, carry):
        r0 = pl.multiple_of(i * CONV_ROWS, CONV_ROWS)
        for cb in range(cdim // LANES):
            cols = slice(cb * LANES, (cb + 1) * LANES)
            win = [ubuf[pl.ds(r0 + SUBLANES * g, SUBLANES), cols] for g in range(groups + 4)]
            out = [jnp.broadcast_to(bdw_ref[:, cols], (SUBLANES, LANES)) for _ in range(groups)]
            for res in range(SUBLANES):
                taps = [j for j in range(CONV_WIDTH) if (shift0 + j) % SUBLANES == res]
                wts = [wb_scr[j * SUBLANES:(j + 1) * SUBLANES, cols] for j in taps]
                part = []
                for g in range(groups + (res > 0)):
                    acc = None
                    for j, w in zip(taps, wts):
                        term = w * win[g + (shift0 + j) // SUBLANES]
                        acc = term if acc is None else acc + term
                    part.append(acc)
                for g in range(groups):
                    if res == 0:
                        out[g] = out[g] + part[g]
                    else:
                        sel = jnp.where(sub >= res, part[g], part[g + 1])
                        out[g] = out[g] + pltpu.roll(sel, SUBLANES - res, axis=0)
            for g in range(groups):
                c_scr[pl.ds(r0 + SUBLANES * g, SUBLANES), cols] = out[g]
        return carry

    lax.fori_loop(0, tq // CONV_ROWS, conv_step, 0)

    @pl.when(t == nt - 1)
    def _():
        cst_ref[0] = ubuf[HIST + tq - (CONV_WIDTH - 1):HIST + tq, :]

    ubuf[0:HIST, :] = ubuf[tq:tq + HIST, :]

    cact = _silu(_layernorm(c_scr[...], lng_ref[...], lnb_ref[...])) * _silu(proj(2 * cdim, cdim))
    yc_ref[0] = _dot(cact.astype(BF16), wco_ref[...])

    qb_ref[0] = (proj(o_q, adim) * ATTN_SCALE).astype(BF16)
    ga_ref[0] = _silu(proj(o_q + 3 * adim, adim))
    kt = _dot_nt(wkvt_ref[0:adim, :], h_scr[...])
    kt32_ref[0] = kt
    ktb_ref[0, 0] = kt.astype(BF16)
    vt = _dot_nt(wkvt_ref[adim:2 * adim, :], h_scr[...])
    vt32_ref[0] = vt
    vtb_ref[0, 0] = vt.astype(BF16)

    tri = tri_scr[...]
    lft = _log_sigmoid(_dot_nt(wft_ref[...], h_scr[...]) + bft_ref[...])[:N_HEADS, :]
    lft_ref[0] = lft
    cst = carryt_scr[...]
    for part in _split3(lft):
        cst = cst + _dot_nt(part, tri)
    carryt_scr[...] = cst[:, tq - 1:tq]
    cumt_ref[0, 0] = cst
    lane = lax.broadcasted_iota(jnp.int32, (tq, LANES), 1)
    lf = _log_sigmoid(_dot(h_scr[...], wf_ref[...]) + bf_ref[...])
    lf = jnp.where(lane < N_HEADS, lf, 0.0)
    cs = carry_scr[...]
    for part in _split3(lf):
        cs = cs + _dot(tri, part)
    carry_scr[...] = cs[tq - 1:tq, :]
    cum_ref[0] = cs[:, :N_HEADS]


def _proj_conv(x, layer, p, *, tq):
    bsz, seq, d = x.shape
    cdim = p["w_dw"].shape[-1]
    adim = N_HEADS * HEAD_DIM
    nt = seq // tq
    assert seq % tq == 0 and tq % CONV_ROWS == 0 and tq >= HIST
    tile = lambda w: pl.BlockSpec((1, tq, w), lambda b, t: (b, t, 0))
    tile_t = lambda r: pl.BlockSpec((1, r, tq), lambda b, t: (b, 0, t))
    chunk_t = lambda r: pl.BlockSpec((1, 1, r, tq), lambda b, t: (b, t, 0, 0))
    out_shape = (
        jax.ShapeDtypeStruct((bsz, adim, seq), F32),
        jax.ShapeDtypeStruct((bsz, adim, seq), F32),
        jax.ShapeDtypeStruct((bsz, N_HEADS, seq), F32),
        jax.ShapeDtypeStruct((bsz, seq, adim), BF16),
        jax.ShapeDtypeStruct((bsz, nt, adim, tq), BF16),
        jax.ShapeDtypeStruct((bsz, nt, adim, tq), BF16),
        jax.ShapeDtypeStruct((bsz, seq, adim), F32),
        jax.ShapeDtypeStruct((bsz, seq, N_HEADS), F32),
        jax.ShapeDtypeStruct((bsz, nt, N_HEADS, tq), F32),
        jax.ShapeDtypeStruct((bsz, seq, d), F32),
        jax.ShapeDtypeStruct((bsz, CONV_WIDTH - 1, cdim), F32),
    )
    out_specs = (
        tile_t(adim), tile_t(adim), tile_t(N_HEADS), tile(adim), chunk_t(adim), chunk_t(adim), tile(adim),
        tile(N_HEADS), chunk_t(N_HEADS), tile(d),
        pl.BlockSpec((1, CONV_WIDTH - 1, cdim), lambda b, t: (b, 0, 0)),
    )
    names = ("norm_g", "w_main", "w_kvt", "w_f", "b_f", "w_ft", "b_ft", "w_dw", "b_dw", "ln_g", "ln_b",
             "w_conv_out")
    in_specs = [tile(d)] + [_const_spec(p[n].shape, layer) for n in names]
    scratch = [
        pltpu.VMEM((tq, d), BF16),
        pltpu.VMEM((HIST + tq, cdim), F32),
        pltpu.VMEM((tq, cdim), F32),
        pltpu.VMEM((CONV_WIDTH * SUBLANES, cdim), F32),
        pltpu.VMEM((tq, tq), BF16),
        pltpu.VMEM((1, LANES), F32),
        pltpu.VMEM((N_HEADS, 1), F32),
    ]
    return pl.pallas_call(
        functools.partial(_proj_conv_kernel, tq=tq, cdim=cdim, adim=adim),
        grid=(bsz, nt), in_specs=in_specs, out_specs=out_specs, out_shape=out_shape, scratch_shapes=scratch,
        compiler_params=pltpu.CompilerParams(dimension_semantics=("arbitrary", "arbitrary"),
                                             vmem_limit_bytes=VMEM_LIMIT),
        name="proj_conv",
    )(x, *(p[n] for n in names))


def _proj_conv_step_kernel(x_ref, st_ref, g_ref, wmain_ref, wf_ref, bf_ref, wdw_ref, bdw_ref, lng_ref, lnb_ref,
                           wco_ref, k32_ref, v32_ref, logf_ref, qb_ref, ga_ref, yc_ref, u_ref, *, cdim, adim):
    h = _rmsnorm(x_ref[...], g_ref[...]).astype(BF16)

    def proj(off, width):
        return _dot(h, wmain_ref[:, off:off + width])

    o_q = 3 * cdim
    u = proj(0, cdim) * _sigmoid(proj(cdim, cdim))
    u_ref[...] = u
    c = wdw_ref[CONV_WIDTH - 1:CONV_WIDTH, :] * u + bdw_ref[...]
    for j in range(CONV_WIDTH - 1):
        c = c + wdw_ref[j:j + 1, :] * st_ref[j]
    cact = _silu(_layernorm(c, lng_ref[...], lnb_ref[...])) * _silu(proj(2 * cdim, cdim))
    yc_ref[...] = _dot(cact.astype(BF16), wco_ref[...])
    qb_ref[...] = (proj(o_q, adim) * ATTN_SCALE).astype(BF16)
    k32_ref[...] = proj(o_q + adim, adim)
    v32_ref[...] = proj(o_q + 2 * adim, adim)
    ga_ref[...] = _silu(proj(o_q + 3 * adim, adim))
    lf = _log_sigmoid(_dot(h, wf_ref[...]) + bf_ref[...])
    logf_ref[...] = lf[:, :N_HEADS]


def _proj_conv_step(x, state_t, layer, p):
    rows, d = x.shape
    cdim = p["w_dw"].shape[-1]
    adim = N_HEADS * HEAD_DIM
    full = lambda shape: pl.BlockSpec(shape, lambda i: (0,) * len(shape))
    out_shape = (
        jax.ShapeDtypeStruct((rows, adim), F32), jax.ShapeDtypeStruct((rows, adim), F32),
        jax.ShapeDtypeStruct((rows, N_HEADS), F32), jax.ShapeDtypeStruct((rows, adim), BF16),
        jax.ShapeDtypeStruct((rows, adim), F32), jax.ShapeDtypeStruct((rows, d), F32),
        jax.ShapeDtypeStruct((rows, cdim), F32),
    )
    names = ("norm_g", "w_main", "w_f", "b_f", "w_dw", "b_dw", "ln_g", "ln_b", "w_conv_out")
    in_specs = [full(x.shape), _const_spec(state_t.shape, layer)] + [_const_spec(p[n].shape, layer) for n in names]
    return pl.pallas_call(
        functools.partial(_proj_conv_step_kernel, cdim=cdim, adim=adim),
        grid=(1,), in_specs=in_specs, out_specs=tuple(full(s.shape) for s in out_shape), out_shape=out_shape,
        compiler_params=pltpu.CompilerParams(dimension_semantics=("arbitrary",), vmem_limit_bytes=VMEM_LIMIT),
        name="proj_conv_step",
    )(x, state_t, *(p[n] for n in names))


def _fox_prompt_kernel(q_ref, kt_ref, vt_ref, cum_ref, cumt_ref, ga_ref, og_ref, acc_scr, *, tq, tk):
    hp = pl.program_id(1)
    qi = pl.program_id(2)
    q0 = qi * tq
    n_full = lax.div(q0, tk)

    q2 = q_ref[0]
    lane = lax.broadcasted_iota(jnp.int32, (1, LANES), 1)
    lower = lane < HEAD_DIM
    zero = jnp.zeros_like(q2)
    qs = (jnp.where(lower, q2, zero), jnp.where(lower, zero, q2))
    cq_all = cum_ref[0]
    hlane = lax.broadcasted_iota(jnp.int32, (1, N_HEADS), 1)
    cqs = tuple(jnp.sum(jnp.where(hlane == HEADS_PER_BLOCK * hp + e, cq_all, 0.0), axis=1, keepdims=True)
                for e in range(HEADS_PER_BLOCK))
    acc_scr[...] = jnp.zeros_like(acc_scr)

    def chunk(ki, state, masked):
        kt = kt_ref[0, ki]
        vt = vt_ref[0, ki]
        new = []
        for e in range(HEADS_PER_BLOCK):
            m_old, l_old = state[2 * e], state[2 * e + 1]
            ck = cumt_ref[0, ki, pl.ds(HEADS_PER_BLOCK * hp + e, 1), :]
            s = _dot(qs[e], kt) - ck
            if masked:
                row = q0 + lax.broadcasted_iota(jnp.int32, (tq, tk), 0)
                col = ki * tk + lax.broadcasted_iota(jnp.int32, (tq, tk), 1)
                s = jnp.where(col <= row, s, NEG)
            m_new = jnp.maximum(m_old, jnp.max(s, axis=1, keepdims=True) + cqs[e])
            p = jnp.exp(s - (m_new - cqs[e]))
            alpha = jnp.exp(m_old - m_new)
            l_new = alpha * l_old + jnp.sum(p, axis=1, keepdims=True)
            acc_scr[e] = alpha * acc_scr[e] + _dot_nt(p.astype(BF16), vt)
            new += [m_new, l_new]
        return tuple(new)

    init = (jnp.full((tq, 1), NEG, F32), jnp.zeros((tq, 1), F32)) * HEADS_PER_BLOCK
    state = lax.fori_loop(0, n_full, lambda ki, st: chunk(ki, st, False), init)
    state = chunk(n_full, state, True)

    o = jnp.where(lower, acc_scr[0] / state[1], acc_scr[1] / state[3])
    og_ref[0] = (o * ga_ref[0]).astype(BF16)


def _fox_prompt(qb, ktb, vtb, cum, cumt, ga, *, tq):
    bsz, seq, adim = qb.shape
    nk, tk = cumt.shape[1], cumt.shape[3]
    assert seq % tq == 0 and tk % tq == 0 and nk * tk == seq
    nhp = adim // LANES
    qtile = pl.BlockSpec((1, tq, LANES), lambda b, hp, qi: (b, qi, hp))
    kvseq = pl.BlockSpec((1, nk, LANES, tk), lambda b, hp, qi: (b, 0, hp, 0))
    return pl.pallas_call(
        functools.partial(_fox_prompt_kernel, tq=tq, tk=tk),
        grid=(bsz, nhp, seq // tq),
        in_specs=[qtile, kvseq, kvseq,
                  pl.BlockSpec((1, tq, N_HEADS), lambda b, hp, qi: (b, qi, 0)),
                  pl.BlockSpec((1, nk, N_HEADS, tk), lambda b, hp, qi: (b, 0, 0, 0)),
                  qtile],
        out_specs=qtile,
        out_shape=jax.ShapeDtypeStruct((bsz, seq, adim), BF16),
        scratch_shapes=[pltpu.VMEM((HEADS_PER_BLOCK, tq, LANES), F32)],
        compiler_params=pltpu.CompilerParams(dimension_semantics=("arbitrary",) * 3, vmem_limit_bytes=VMEM_LIMIT),
        name="fox_prompt",
    )(qb, ktb, vtb, cum, cumt, ga)


def _fox_decode_kernel(pt_ref, q_ref, kn_ref, vn_ref, lfn_ref, ga_ref, *rest, pages):
    k_refs = rest[:pages]
    v_refs = rest[pages:2 * pages]
    lf_refs = rest[2 * pages:3 * pages]
    og_ref, m_scr, l_scr, acc_scr, carry_scr = rest[3 * pages:]
    del pt_ref
    j = pl.program_id(1)
    adim = N_HEADS * HEAD_DIM

    sub = lax.broadcasted_iota(jnp.int32, (N_HEADS, adim), 0)
    lane_head = lax.broadcasted_iota(jnp.int32, (N_HEADS, adim), 1) // HEAD_DIM
    own = sub == lane_head
    qbd32 = jnp.where(own, jnp.broadcast_to(q_ref[0].astype(F32), (N_HEADS, adim)), 0.0)
    qbd = qbd32.astype(BF16)

    @pl.when(j == 0)
    def _():
        kn = kn_ref[0].astype(BF16).astype(F32)
        m_scr[...] = jnp.sum(qbd32 * kn, axis=1, keepdims=True)
        l_scr[...] = jnp.ones_like(l_scr)
        acc_scr[...] = jnp.broadcast_to(vn_ref[0].astype(BF16).astype(F32), (N_HEADS, adim))
        hs = lax.broadcasted_iota(jnp.int32, (N_HEADS, N_HEADS), 0)
        hl = lax.broadcasted_iota(jnp.int32, (N_HEADS, N_HEADS), 1)
        carry_scr[...] = jnp.sum(jnp.where(hs == hl, jnp.broadcast_to(lfn_ref[0], (N_HEADS, N_HEADS)), 0.0),
                                 axis=1, keepdims=True)

    lf_all = jnp.concatenate([r[...] for r in lf_refs], axis=0)
    jr = lax.broadcasted_iota(jnp.int32, (PAGE_SIZE, PAGE_SIZE), 0)
    sc = lax.broadcasted_iota(jnp.int32, (PAGE_SIZE, PAGE_SIZE), 1)
    later = jnp.where(jr > sc, 1.0, 0.0).astype(BF16)
    within = None
    for part in _split3(lf_all):
        w = _dot(part, later)
        within = w if within is None else within + w

    run = carry_scr[...]
    s_parts = [None] * pages
    for i in reversed(range(pages)):
        bias = within[i * N_HEADS:(i + 1) * N_HEADS, :] + run
        s_parts[i] = _dot(qbd, k_refs[i][...].astype(BF16)) + bias
        run = run + jnp.sum(lf_refs[i][...], axis=1, keepdims=True)
    carry_scr[...] = run

    s = jnp.concatenate(s_parts, axis=1)
    m_old = m_scr[...]
    m_new = jnp.maximum(m_old, jnp.max(s, axis=1, keepdims=True))
    alpha = jnp.exp(m_old - m_new)
    p = jnp.exp(s - m_new)
    l_scr[...] = alpha * l_scr[...] + jnp.sum(p, axis=1, keepdims=True)
    pv = jnp.zeros((N_HEADS, adim), F32)
    for i in range(pages):
        pv = pv + _dot_nt(p[:, i * PAGE_SIZE:(i + 1) * PAGE_SIZE].astype(BF16), v_refs[i][...].astype(BF16))
    acc_scr[...] = alpha * acc_scr[...] + pv
    m_scr[...] = m_new

    @pl.when(j == pl.num_programs(1) - 1)
    def _():
        o = jnp.sum(jnp.where(own, acc_scr[...] / l_scr[...], 0.0), axis=0, keepdims=True)
        og_ref[0] = (o * ga_ref[0]).astype(BF16)


def _fox_decode(qb, k_new, v_new, lf_new, ga, cache_kt, cache_vt, cache_lft, page_table, layer, *, pages):
    db, adim = qb.shape
    n_pages = page_table.shape[1]
    assert n_pages % pages == 0
    steps = n_pages // pages
    row = lambda w: pl.BlockSpec((1, 1, w), lambda b, j, pt: (b, 0, 0))

    def page_spec(i, rows):
        return pl.BlockSpec((None, None, rows, PAGE_SIZE),
                            lambda b, j, pt: (layer, pt[b, (steps - 1 - j) * pages + i], 0, 0))

    in_specs = ([row(adim), row(adim), row(adim), row(N_HEADS), row(adim)]
                + [page_spec(i, adim) for i in range(pages)]
                + [page_spec(i, adim) for i in range(pages)]
                + [page_spec(i, N_HEADS) for i in range(pages)])
    grid_spec = pltpu.PrefetchScalarGridSpec(
        num_scalar_prefetch=1, grid=(db, steps), in_specs=in_specs, out_specs=row(adim),
        scratch_shapes=[pltpu.VMEM((N_HEADS, 1), F32), pltpu.VMEM((N_HEADS, 1), F32),
                        pltpu.VMEM((N_HEADS, adim), F32), pltpu.VMEM((N_HEADS, 1), F32)])
    r3 = lambda a: a.reshape(db, 1, a.shape[-1])
    og = pl.pallas_call(
        functools.partial(_fox_decode_kernel, pages=pages),
        grid_spec=grid_spec, out_shape=jax.ShapeDtypeStruct((db, 1, adim), BF16),
        compiler_params=pltpu.CompilerParams(dimension_semantics=("arbitrary", "arbitrary"),
                                             vmem_limit_bytes=VMEM_LIMIT),
        name="fox_decode",
    )(page_table, r3(qb), r3(k_new), r3(v_new), r3(lf_new), r3(ga),
      *([cache_kt] * pages), *([cache_vt] * pages), *([cache_lft] * pages))
    return og.reshape(db, adim)


def _merge_out_kernel(x_ref, og_ref, yc_ref, g_ref, wmix_ref, wao_ref, wo_ref, fg_ref, out_ref, *, final):
    x = x_ref[0]
    d = x.shape[-1]
    h = _rmsnorm(x, g_ref[...]).astype(BF16)
    mix_c = _sigmoid(_dot(h, wmix_ref[:, 0:d]))
    mix_a = _sigmoid(_dot(h, wmix_ref[:, d:2 * d]))
    y_a = _dot(og_ref[0], wao_ref[...])
    m = mix_c * yc_ref[0] + mix_a * y_a
    y = x + _dot(m.astype(BF16), wo_ref[...])
    if final:
        y = _rmsnorm(y, fg_ref[...])
    out_ref[0] = y


def _merge_out(x, og, yc, layer, p, final_g, *, tq, final):
    bsz, seq, d = x.shape
    adim = og.shape[-1]
    assert seq % tq == 0
    tile = lambda w: pl.BlockSpec((1, tq, w), lambda b, t: (b, t, 0))
    in_specs = [tile(d), tile(adim), tile(d)] + [_const_spec(p[n].shape, layer) for n in
                                                  ("norm_g", "w_mix", "w_attn_out", "w_o")]
    in_specs.append(pl.BlockSpec(final_g.shape, lambda b, t: (0, 0)))
    return pl.pallas_call(
        functools.partial(_merge_out_kernel, final=final),
        grid=(bsz, seq // tq), in_specs=in_specs, out_specs=tile(d),
        out_shape=jax.ShapeDtypeStruct((bsz, seq, d), F32),
        compiler_params=pltpu.CompilerParams(dimension_semantics=("arbitrary", "arbitrary"),
                                             vmem_limit_bytes=VMEM_LIMIT),
        name="merge_out",
    )(x, og, yc, p["norm_g"], p["w_mix"], p["w_attn_out"], p["w_o"], final_g)


PROMPT_TILE = 512
ATTN_Q_TILE = 256
DECODE_PAGES = 8


def kernel(x_prompt, x_sample, cache_k, cache_v, cache_logf, state_conv, page_table, norm_g, w_in, b_f, w_dw,
           b_dw, ln_g, ln_b, w_conv_out, w_attn_out, w_o, final_g):
    depth, d = norm_g.shape
    bsz, seq, _ = x_prompt.shape
    db, dec_seq, _ = x_sample.shape
    assert dec_seq == 1
    cdim = w_dw.shape[-1]
    adim = N_HEADS * HEAD_DIM
    o_k = 3 * cdim + adim
    n_main = 3 * cdim + 4 * adim

    w_in_b = w_in.astype(BF16)
    w_f = w_in_b[:, :, n_main:n_main + N_HEADS]
    params = {
        "norm_g": norm_g.reshape(depth, 1, d),
        "w_main": w_in_b[:, :, :n_main],
        "w_kvt": jnp.swapaxes(w_in_b[:, :, o_k:o_k + 2 * adim], 1, 2),
        "w_f": jnp.pad(w_f, ((0, 0), (0, 0), (0, LANES - N_HEADS))),
        "w_ft": jnp.pad(jnp.swapaxes(w_f, 1, 2), ((0, 0), (0, BF16_ROWS - N_HEADS), (0, 0))),
        "w_mix": w_in_b[:, :, n_main + N_HEADS:],
        "b_f": jnp.pad(b_f, ((0, 0), (0, LANES - N_HEADS))).reshape(depth, 1, LANES),
        "b_ft": jnp.pad(b_f, ((0, 0), (0, BF16_ROWS - N_HEADS))).reshape(depth, BF16_ROWS, 1),
        "w_dw": w_dw,
        "b_dw": b_dw.reshape(depth, 1, cdim),
        "ln_g": ln_g.reshape(depth, 1, cdim),
        "ln_b": ln_b.reshape(depth, 1, cdim),
        "w_conv_out": w_conv_out.astype(BF16),
        "w_attn_out": w_attn_out.astype(BF16),
        "w_o": w_o.astype(BF16),
    }
    fg = final_g.reshape(1, d)
    n_pool = cache_k.shape[1]
    ckt = jnp.transpose(cache_k, (0, 1, 3, 4, 2)).reshape(depth, n_pool, adim, PAGE_SIZE)
    cvt = jnp.transpose(cache_v, (0, 1, 3, 4, 2)).reshape(depth, n_pool, adim, PAGE_SIZE)
    clft = jnp.transpose(cache_logf, (0, 1, 3, 2))
    state_t = jnp.transpose(state_conv, (0, 2, 1, 3))

    def heads_last(a_t, n):
        return jnp.transpose(a_t.reshape(n, N_HEADS, HEAD_DIM, a_t.shape[-1]), (0, 3, 1, 2))

    xp = x_prompt
    xs = x_sample.reshape(db, d)
    kp, vp, lp, cp, ksl, vsl, lsl, csl = ([] for _ in range(8))
    for layer in range(depth):
        final = layer == depth - 1
        kt32, vt32, lft, qb, ktb, vtb, ga, cum, cumt, yc, cst = _proj_conv(xp, layer, params, tq=PROMPT_TILE)
        og = _fox_prompt(qb, ktb, vtb, cum, cumt, ga, tq=ATTN_Q_TILE)
        xp = _merge_out(xp, og, yc, layer, params, fg, tq=PROMPT_TILE, final=final)
        kp.append(heads_last(kt32, bsz))
        vp.append(heads_last(vt32, bsz))
        lp.append(jnp.transpose(lft, (0, 2, 1)))
        cp.append(cst)

        k_s, v_s, lf_s, q_s, ga_s, yc_s, u_s = _proj_conv_step(xs, state_t, layer, params)
        og_s = _fox_decode(q_s, k_s, v_s, lf_s, ga_s, ckt, cvt, clft, page_table, layer, pages=DECODE_PAGES)
        xs = _merge_out(xs[None], og_s[None], yc_s[None], layer, params, fg, tq=db, final=final)[0]
        ksl.append(k_s.reshape(db, 1, N_HEADS, HEAD_DIM))
        vsl.append(v_s.reshape(db, 1, N_HEADS, HEAD_DIM))
        lsl.append(lf_s.reshape(db, 1, N_HEADS))
        csl.append(jnp.transpose(jnp.concatenate([state_t[layer, 1:], u_s[None]], axis=0), (1, 0, 2)))

    return (xp, xs.reshape(db, 1, d), jnp.stack(kp), jnp.stack(vp), jnp.stack(lp), jnp.stack(cp),
            jnp.stack(ksl), jnp.stack(vsl), jnp.stack(lsl), jnp.stack(csl))
```

```python
import functools

import jax
import jax.numpy as jnp
from jax import lax
from jax.experimental import pallas as pl
from jax.experimental.pallas import tpu as pltpu

CONV_WIDTH = 31
N_HEADS = 8
HEAD_DIM = 64
PAGE_SIZE = 128
RMS_EPS = 1e-6
LN_EPS = 1e-5
ATTN_SCALE = HEAD_DIM ** -0.5

LANES = 128
SUBLANES = 8
BF16_ROWS = 16
HEADS_PER_BLOCK = LANES // HEAD_DIM
VMEM_LIMIT = 56 * 1024 * 1024

HIST = 32
CONV_ROWS = 64
NEG = -1e30
PRUNE_T = 90.0

BF16 = jnp.bfloat16
F32 = jnp.float32


def _sigmoid(x):
    return 1.0 / (1.0 + jnp.exp(-x))


def _silu(x):
    return x * _sigmoid(x)


def _log_sigmoid(x):
    return jnp.minimum(x, 0.0) - jnp.log1p(jnp.exp(-jnp.abs(x)))


def _rmsnorm(x, g):
    return x * lax.rsqrt(jnp.mean(x * x, axis=-1, keepdims=True) + RMS_EPS) * g


def _layernorm(x, g, b):
    mu = jnp.mean(x, axis=-1, keepdims=True)
    xc = x - mu
    var = jnp.mean(xc * xc, axis=-1, keepdims=True)
    return xc * lax.rsqrt(var + LN_EPS) * g + b


def _dot(a, b):
    return jnp.dot(a, b, preferred_element_type=F32)


def _dot_nt(a, b):
    return lax.dot_general(a, b, (((1,), (1,)), ((), ())), preferred_element_type=F32)


def _split3(x):
    hi = x.astype(BF16)
    r = x - hi.astype(F32)
    mid = r.astype(BF16)
    lo = (r - mid.astype(F32)).astype(BF16)
    return hi, mid, lo


def _const_spec(shape, layer):
    nd = len(shape) - 1
    return pl.BlockSpec((None,) + tuple(shape[1:]), lambda *_: (layer,) + (0,) * nd)


def _proj_conv_kernel(x_ref, g_ref, wmain_ref, wkvt_ref, wf_ref, bf_ref, wft_ref, bft_ref, wdw_ref, bdw_ref,
                      lng_ref, lnb_ref, wco_ref,
                      kt32_ref, vt32_ref, lft_ref, qb_ref, ktb_ref, vtb_ref, ga_ref, cum_ref, cumt_ref, yc_ref,
                      cst_ref, kn2_ref, h_scr, ubuf, c_scr, wb_scr, tri_scr, carry_scr, carryt_scr, kn2_scr,
                      *, tq, cdim, adim):
    b = pl.program_id(0)
    t = pl.program_id(1)
    nt = pl.num_programs(1)

    @pl.when((b == 0) & (t == 0))
    def _():
        row = lax.broadcasted_iota(jnp.int32, (tq, tq), 0)
        col = lax.broadcasted_iota(jnp.int32, (tq, tq), 1)
        tri_scr[...] = jnp.where(col <= row, 1.0, 0.0).astype(BF16)
        for j in range(CONV_WIDTH):
            wb_scr[j * SUBLANES:(j + 1) * SUBLANES, :] = jnp.broadcast_to(wdw_ref[j:j + 1, :], (SUBLANES, cdim))

    @pl.when(t == 0)
    def _():
        ubuf[0:HIST, :] = jnp.zeros((HIST, cdim), F32)
        carry_scr[...] = jnp.zeros_like(carry_scr)
        carryt_scr[...] = jnp.zeros_like(carryt_scr)
        kn2_scr[...] = jnp.zeros_like(kn2_scr)

    x = x_ref[0]
    h_scr[...] = _rmsnorm(x, g_ref[...]).astype(BF16)

    def proj(off, width):
        return _dot(h_scr[...], wmain_ref[:, off:off + width])

    o_q = 3 * cdim
    u = proj(0, cdim) * _sigmoid(proj(cdim, cdim))
    ubuf[HIST:HIST + tq, :] = u

    shift0 = HIST - (CONV_WIDTH - 1)
    groups = CONV_ROWS // SUBLANES
    sub = lax.broadcasted_iota(jnp.int32, (SUBLANES, LANES), 0)

    def conv_step(i, carry):
        r0 = pl.multiple_of(i * CONV_ROWS, CONV_ROWS)
        for cb in range(cdim // LANES):
            cols = slice(cb * LANES, (cb + 1) * LANES)
            win = [ubuf[pl.ds(r0 + SUBLANES * g, SUBLANES), cols] for g in range(groups + 4)]
            out = [jnp.broadcast_to(bdw_ref[:, cols], (SUBLANES, LANES)) for _ in range(groups)]
            for res in range(SUBLANES):
                taps = [j for j in range(CONV_WIDTH) if (shift0 + j) % SUBLANES == res]
                wts = [wb_scr[j * SUBLANES:(j + 1) * SUBLANES, cols] for j in taps]
                part = []
                for g in range(groups + (res > 0)):
                    acc = None
                    for j, w in zip(taps, wts):
                        term = w * win[g + (shift0 + j) // SUBLANES]
                        acc = term if acc is None else acc + term
                    part.append(acc)
                for g in range(groups):
                    if res == 0:
                        out[g] = out[g] + part[g]
                    else:
                        sel = jnp.where(sub >= res, part[g], part[g + 1])
                        out[g] = out[g] + pltpu.roll(sel, SUBLANES - res, axis=0)
            for g in range(groups):
                c_scr[pl.ds(r0 + SUBLANES * g, SUBLANES), cols] = out[g]
        return carry

    lax.fori_loop(0, tq // CONV_ROWS, conv_step, 0)

    @pl.when(t == nt - 1)
    def _():
        cst_ref[0] = ubuf[HIST + tq - (CONV_WIDTH - 1):HIST + tq, :]

    ubuf[0:HIST, :] = ubuf[tq:tq + HIST, :]

    cact = _silu(_layernorm(c_scr[...], lng_ref[...], lnb_ref[...])) * _silu(proj(2 * cdim, cdim))
    yc_ref[0] = _dot(cact.astype(BF16), wco_ref[...])

    qb_ref[0] = (proj(o_q, adim) * ATTN_SCALE).astype(BF16)
    ga_ref[0] = _silu(proj(o_q + 3 * adim, adim))
    kt = _dot_nt(wkvt_ref[0:adim, :], h_scr[...])
    kt32_ref[0] = kt
    kb = kt.astype(BF16)
    ktb_ref[0, 0] = kb
    kf = kb.astype(F32)
    ksq = kf * kf
    hrow = lax.broadcasted_iota(jnp.int32, (N_HEADS, LANES), 0)
    kn2 = kn2_scr[...]
    for hd in range(N_HEADS):
        col = jnp.sum(ksq[hd * HEAD_DIM:(hd + 1) * HEAD_DIM, :], axis=0, keepdims=True)
        top = jnp.max(col, axis=1, keepdims=True)
        kn2 = jnp.where(hrow == hd, jnp.maximum(kn2, top), kn2)
    kn2_scr[...] = kn2
    kn2_ref[0, 0] = kn2
    vt = _dot_nt(wkvt_ref[adim:2 * adim, :], h_scr[...])
    vt32_ref[0] = vt
    vtb_ref[0, 0] = vt.astype(BF16)

    tri = tri_scr[...]
    lft = _log_sigmoid(_dot_nt(wft_ref[...], h_scr[...]) + bft_ref[...])[:N_HEADS, :]
    lft_ref[0] = lft
    cst = carryt_scr[...]
    for part in _split3(lft):
        cst = cst + _dot_nt(part, tri)
    carryt_scr[...] = cst[:, tq - 1:tq]
    cumt_ref[0, 0] = cst
    lane = lax.broadcasted_iota(jnp.int32, (tq, LANES), 1)
    lf = _log_sigmoid(_dot(h_scr[...], wf_ref[...]) + bf_ref[...])
    lf = jnp.where(lane < N_HEADS, lf, 0.0)
    cs = carry_scr[...]
    for part in _split3(lf):
        cs = cs + _dot(tri, part)
    carry_scr[...] = cs[tq - 1:tq, :]
    cum_ref[0] = cs[:, :N_HEADS]


def _proj_conv(x, layer, p, *, tq):
    bsz, seq, d = x.shape
    cdim = p["w_dw"].shape[-1]
    adim = N_HEADS * HEAD_DIM
    nt = seq // tq
    assert seq % tq == 0 and tq % CONV_ROWS == 0 and tq >= HIST
    tile = lambda w: pl.BlockSpec((1, tq, w), lambda b, t: (b, t, 0))
    tile_t = lambda r: pl.BlockSpec((1, r, tq), lambda b, t: (b, 0, t))
    chunk_t = lambda r: pl.BlockSpec((1, 1, r, tq), lambda b, t: (b, t, 0, 0))
    out_shape = (
        jax.ShapeDtypeStruct((bsz, adim, seq), F32),
        jax.ShapeDtypeStruct((bsz, adim, seq), F32),
        jax.ShapeDtypeStruct((bsz, N_HEADS, seq), F32),
        jax.ShapeDtypeStruct((bsz, seq, adim), BF16),
        jax.ShapeDtypeStruct((bsz, nt, adim, tq), BF16),
        jax.ShapeDtypeStruct((bsz, nt, adim, tq), BF16),
        jax.ShapeDtypeStruct((bsz, seq, adim), F32),
        jax.ShapeDtypeStruct((bsz, seq, N_HEADS), F32),
        jax.ShapeDtypeStruct((bsz, nt, N_HEADS, tq), F32),
        jax.ShapeDtypeStruct((bsz, seq, d), F32),
        jax.ShapeDtypeStruct((bsz, CONV_WIDTH - 1, cdim), F32),
        jax.ShapeDtypeStruct((bsz, nt, N_HEADS, LANES), F32),
    )
    out_specs = (
        tile_t(adim), tile_t(adim), tile_t(N_HEADS), tile(adim), chunk_t(adim), chunk_t(adim), tile(adim),
        tile(N_HEADS), chunk_t(N_HEADS), tile(d),
        pl.BlockSpec((1, CONV_WIDTH - 1, cdim), lambda b, t: (b, 0, 0)),
        pl.BlockSpec((1, 1, N_HEADS, LANES), lambda b, t: (b, t, 0, 0)),
    )
    names = ("norm_g", "w_main", "w_kvt", "w_f", "b_f", "w_ft", "b_ft", "w_dw", "b_dw", "ln_g", "ln_b",
             "w_conv_out")
    in_specs = [tile(d)] + [_const_spec(p[n].shape, layer) for n in names]
    scratch = [
        pltpu.VMEM((tq, d), BF16),
        pltpu.VMEM((HIST + tq, cdim), F32),
        pltpu.VMEM((tq, cdim), F32),
        pltpu.VMEM((CONV_WIDTH * SUBLANES, cdim), F32),
        pltpu.VMEM((tq, tq), BF16),
        pltpu.VMEM((1, LANES), F32),
        pltpu.VMEM((N_HEADS, 1), F32),
        pltpu.VMEM((N_HEADS, LANES), F32),
    ]
    return pl.pallas_call(
        functools.partial(_proj_conv_kernel, tq=tq, cdim=cdim, adim=adim),
        grid=(bsz, nt), in_specs=in_specs, out_specs=out_specs, out_shape=out_shape, scratch_shapes=scratch,
        compiler_params=pltpu.CompilerParams(dimension_semantics=("arbitrary", "arbitrary"),
                                             vmem_limit_bytes=VMEM_LIMIT),
        name="proj_conv",
    )(x, *(p[n] for n in names))


def _proj_conv_step_kernel(x_ref, st_ref, g_ref, wmain_ref, wf_ref, bf_ref, wdw_ref, bdw_ref, lng_ref, lnb_ref,
                           wco_ref, k32_ref, v32_ref, logf_ref, qb_ref, ga_ref, yc_ref, u_ref, *, cdim, adim):
    h = _rmsnorm(x_ref[...], g_ref[...]).astype(BF16)

    def proj(off, width):
        return _dot(h, wmain_ref[:, off:off + width])

    o_q = 3 * cdim
    u = proj(0, cdim) * _sigmoid(proj(cdim, cdim))
    u_ref[...] = u
    c = wdw_ref[CONV_WIDTH - 1:CONV_WIDTH, :] * u + bdw_ref[...]
    for j in range(CONV_WIDTH - 1):
        c = c + wdw_ref[j:j + 1, :] * st_ref[j]
    cact = _silu(_layernorm(c, lng_ref[...], lnb_ref[...])) * _silu(proj(2 * cdim, cdim))
    yc_ref[...] = _dot(cact.astype(BF16), wco_ref[...])
    qb_ref[...] = (proj(o_q, adim) * ATTN_SCALE).astype(BF16)
    k32_ref[...] = proj(o_q + adim, adim)
    v32_ref[...] = proj(o_q + 2 * adim, adim)
    ga_ref[...] = _silu(proj(o_q + 3 * adim, adim))
    lf = _log_sigmoid(_dot(h, wf_ref[...]) + bf_ref[...])
    logf_ref[...] = lf[:, :N_HEADS]


def _proj_conv_step(x, state_t, layer, p):
    rows, d = x.shape
    cdim = p["w_dw"].shape[-1]
    adim = N_HEADS * HEAD_DIM
    full = lambda shape: pl.BlockSpec(shape, lambda i: (0,) * len(shape))
    out_shape = (
        jax.ShapeDtypeStruct((rows, adim), F32), jax.ShapeDtypeStruct((rows, adim), F32),
        jax.ShapeDtypeStruct((rows, N_HEADS), F32), jax.ShapeDtypeStruct((rows, adim), BF16),
        jax.ShapeDtypeStruct((rows, adim), F32), jax.ShapeDtypeStruct((rows, d), F32),
        jax.ShapeDtypeStruct((rows, cdim), F32),
    )
    names = ("norm_g", "w_main", "w_f", "b_f", "w_dw", "b_dw", "ln_g", "ln_b", "w_conv_out")
    in_specs = [full(x.shape), _const_spec(state_t.shape, layer)] + [_const_spec(p[n].shape, layer) for n in names]
    return pl.pallas_call(
        functools.partial(_proj_conv_step_kernel, cdim=cdim, adim=adim),
        grid=(1,), in_specs=in_specs, out_specs=tuple(full(s.shape) for s in out_shape), out_shape=out_shape,
        compiler_params=pltpu.CompilerParams(dimension_semantics=("arbitrary",), vmem_limit_bytes=VMEM_LIMIT),
        name="proj_conv_step",
    )(x, state_t, *(p[n] for n in names))


def _fox_prompt_kernel(q_ref, kt_ref, vt_ref, cum_ref, cumt_ref, cumend_ref, kn2_ref, ga_ref, og_ref, acc_scr,
                       *, tq, tk):
    hp = pl.program_id(1)
    qi = pl.program_id(2)
    q0 = qi * tq
    n_full = lax.div(q0, tk)

    q2 = q_ref[0]
    lane = lax.broadcasted_iota(jnp.int32, (1, LANES), 1)
    lower = lane < HEAD_DIM
    zero = jnp.zeros_like(q2)
    qs = (jnp.where(lower, q2, zero), jnp.where(lower, zero, q2))
    cq_all = cum_ref[0]
    hlane = lax.broadcasted_iota(jnp.int32, (1, N_HEADS), 1)
    cqs = tuple(jnp.sum(jnp.where(hlane == HEADS_PER_BLOCK * hp + e, cq_all, 0.0), axis=1, keepdims=True)
                for e in range(HEADS_PER_BLOCK))
    acc_scr[...] = jnp.zeros_like(acc_scr)

    def chunk(ki, state, masked):
        kt = kt_ref[0, ki]
        vt = vt_ref[0, ki]
        new = []
        for e in range(HEADS_PER_BLOCK):
            m_old, l_old = state[2 * e], state[2 * e + 1]
            ck = cumt_ref[0, ki, pl.ds(HEADS_PER_BLOCK * hp + e, 1), :]
            s = _dot(qs[e], kt) - ck
            if masked:
                row = q0 + lax.broadcasted_iota(jnp.int32, (tq, tk), 0)
                col = ki * tk + lax.broadcasted_iota(jnp.int32, (tq, tk), 1)
                s = jnp.where(col <= row, s, NEG)
            m_new = jnp.maximum(m_old, jnp.max(s, axis=1, keepdims=True) + cqs[e])
            p = jnp.exp(s - (m_new - cqs[e]))
            alpha = jnp.exp(m_old - m_new)
            l_new = alpha * l_old + jnp.sum(p, axis=1, keepdims=True)
            acc_scr[e] = alpha * acc_scr[e] + _dot_nt(p.astype(BF16), vt)
            new += [m_new, l_new]
        return tuple(new)

    init = (jnp.full((tq, 1), NEG, F32), jnp.zeros((tq, 1), F32)) * HEADS_PER_BLOCK
    state = chunk(n_full, init, True)

    nk = cumend_ref.shape[-1]
    qf = q2.astype(F32)
    qsq = qf * qf
    kidx = lax.broadcasted_iota(jnp.int32, (1, nk), 1)
    first_live = []
    for e in range(HEADS_PER_BLOCK):
        head = HEADS_PER_BLOCK * hp + e
        own_lanes = lower if e == 0 else jnp.logical_not(lower)
        qn = jnp.sqrt(jnp.sum(jnp.where(own_lanes, qsq, 0.0), axis=1, keepdims=True))
        kn = jnp.sqrt(kn2_ref[0, n_full, pl.ds(head, 1), :][:, 0:1])
        reach = jnp.max(qn * kn + cqs[e] - state[2 * e], axis=0, keepdims=True) + PRUNE_T
        live = (cumend_ref[0, pl.ds(head, 1), :] <= reach) | (kidx >= n_full)
        first_live.append(jnp.min(jnp.where(live, kidx, nk).astype(F32)))
    ki_start = jnp.minimum(first_live[0], first_live[1]).astype(jnp.int32)
    state = lax.fori_loop(ki_start, n_full, lambda ki, st: chunk(ki, st, False), state)

    o = jnp.where(lower, acc_scr[0] / state[1], acc_scr[1] / state[3])
    og_ref[0] = (o * ga_ref[0]).astype(BF16)


def _fox_prompt(qb, ktb, vtb, cum, cumt, kn2, ga, *, tq):
    bsz, seq, adim = qb.shape
    nk, tk = cumt.shape[1], cumt.shape[3]
    assert seq % tq == 0 and tk % tq == 0 and nk * tk == seq
    nhp = adim // LANES
    cumend = jnp.transpose(cumt[:, :, :, tk - 1], (0, 2, 1))
    qtile = pl.BlockSpec((1, tq, LANES), lambda b, hp, qi: (b, qi, hp))
    kvseq = pl.BlockSpec((1, nk, LANES, tk), lambda b, hp, qi: (b, 0, hp, 0))
    return pl.pallas_call(
        functools.partial(_fox_prompt_kernel, tq=tq, tk=tk),
        grid=(bsz, nhp, seq // tq),
        in_specs=[qtile, kvseq, kvseq,
                  pl.BlockSpec((1, tq, N_HEADS), lambda b, hp, qi: (b, qi, 0)),
                  pl.BlockSpec((1, nk, N_HEADS, tk), lambda b, hp, qi: (b, 0, 0, 0)),
                  pl.BlockSpec((1, N_HEADS, nk), lambda b, hp, qi: (b, 0, 0)),
                  pl.BlockSpec((1, nk, N_HEADS, LANES), lambda b, hp, qi: (b, 0, 0, 0)),
                  qtile],
        out_specs=qtile,
        out_shape=jax.ShapeDtypeStruct((bsz, seq, adim), BF16),
        scratch_shapes=[pltpu.VMEM((HEADS_PER_BLOCK, tq, LANES), F32)],
        compiler_params=pltpu.CompilerParams(dimension_semantics=("arbitrary",) * 3, vmem_limit_bytes=VMEM_LIMIT),
        name="fox_prompt",
    )(qb, ktb, vtb, cum, cumt, cumend, kn2, ga)


def _fox_decode_kernel(pt_ref, q_ref, kn_ref, vn_ref, lfn_ref, ga_ref, *rest, pages):
    k_refs = rest[:pages]
    v_refs = rest[pages:2 * pages]
    lf_refs = rest[2 * pages:3 * pages]
    og_ref, m_scr, l_scr, acc_scr, carry_scr = rest[3 * pages:]
    del pt_ref
    j = pl.program_id(1)
    adim = N_HEADS * HEAD_DIM

    sub = lax.broadcasted_iota(jnp.int32, (N_HEADS, adim), 0)
    lane_head = lax.broadcasted_iota(jnp.int32, (N_HEADS, adim), 1) // HEAD_DIM
    own = sub == lane_head
    qbd32 = jnp.where(own, jnp.broadcast_to(q_ref[0].astype(F32), (N_HEADS, adim)), 0.0)
    qbd = qbd32.astype(BF16)

    @pl.when(j == 0)
    def _():
        kn = kn_ref[0].astype(BF16).astype(F32)
        m_scr[...] = jnp.sum(qbd32 * kn, axis=1, keepdims=True)
        l_scr[...] = jnp.ones_like(l_scr)
        acc_scr[...] = jnp.broadcast_to(vn_ref[0].astype(BF16).astype(F32), (N_HEADS, adim))
        hs = lax.broadcasted_iota(jnp.int32, (N_HEADS, N_HEADS), 0)
        hl = lax.broadcasted_iota(jnp.int32, (N_HEADS, N_HEADS), 1)
        carry_scr[...] = jnp.sum(jnp.where(hs == hl, jnp.broadcast_to(lfn_ref[0], (N_HEADS, N_HEADS)), 0.0),
                                 axis=1, keepdims=True)

    lf_all = jnp.concatenate([r[...] for r in lf_refs], axis=0)
    jr = lax.broadcasted_iota(jnp.int32, (PAGE_SIZE, PAGE_SIZE), 0)
    sc = lax.broadcasted_iota(jnp.int32, (PAGE_SIZE, PAGE_SIZE), 1)
    later = jnp.where(jr > sc, 1.0, 0.0).astype(BF16)
    within = None
    for part in _split3(lf_all):
        w = _dot(part, later)
        within = w if within is None else within + w

    run = carry_scr[...]
    s_parts = [None] * pages
    for i in reversed(range(pages)):
        bias = within[i * N_HEADS:(i + 1) * N_HEADS, :] + run
        s_parts[i] = _dot(qbd, k_refs[i][...].astype(BF16)) + bias
        run = run + jnp.sum(lf_refs[i][...], axis=1, keepdims=True)
    carry_scr[...] = run

    s = jnp.concatenate(s_parts, axis=1)
    m_old = m_scr[...]
    m_new = jnp.maximum(m_old, jnp.max(s, axis=1, keepdims=True))
    alpha = jnp.exp(m_old - m_new)
    p = jnp.exp(s - m_new)
    l_scr[...] = alpha * l_scr[...] + jnp.sum(p, axis=1, keepdims=True)
    pv = jnp.zeros((N_HEADS, adim), F32)
    for i in range(pages):
        pv = pv + _dot_nt(p[:, i * PAGE_SIZE:(i + 1) * PAGE_SIZE].astype(BF16), v_refs[i][...].astype(BF16))
    acc_scr[...] = alpha * acc_scr[...] + pv
    m_scr[...] = m_new

    @pl.when(j == pl.num_programs(1) - 1)
    def _():
        o = jnp.sum(jnp.where(own, acc_scr[...] / l_scr[...], 0.0), axis=0, keepdims=True)
        og_ref[0] = (o * ga_ref[0]).astype(BF16)


def _fox_decode(qb, k_new, v_new, lf_new, ga, cache_kt, cache_vt, cache_lft, page_table, layer, *, pages):
    db, adim = qb.shape
    n_pages = page_table.shape[1]
    assert n_pages % pages == 0
    steps = n_pages // pages
    row = lambda w: pl.BlockSpec((1, 1, w), lambda b, j, pt: (b, 0, 0))

    def page_spec(i, rows):
        return pl.BlockSpec((None, None, rows, PAGE_SIZE),
                            lambda b, j, pt: (layer, pt[b, (steps - 1 - j) * pages + i], 0, 0))

    in_specs = ([row(adim), row(adim), row(adim), row(N_HEADS), row(adim)]
                + [page_spec(i, adim) for i in range(pages)]
                + [page_spec(i, adim) for i in range(pages)]
                + [page_spec(i, N_HEADS) for i in range(pages)])
    grid_spec = pltpu.PrefetchScalarGridSpec(
        num_scalar_prefetch=1, grid=(db, steps), in_specs=in_specs, out_specs=row(adim),
        scratch_shapes=[pltpu.VMEM((N_HEADS, 1), F32), pltpu.VMEM((N_HEADS, 1), F32),
                        pltpu.VMEM((N_HEADS, adim), F32), pltpu.VMEM((N_HEADS, 1), F32)])
    r3 = lambda a: a.reshape(db, 1, a.shape[-1])
    og = pl.pallas_call(
        functools.partial(_fox_decode_kernel, pages=pages),
        grid_spec=grid_spec, out_shape=jax.ShapeDtypeStruct((db, 1, adim), BF16),
        compiler_params=pltpu.CompilerParams(dimension_semantics=("arbitrary", "arbitrary"),
                                             vmem_limit_bytes=VMEM_LIMIT),
        name="fox_decode",
    )(page_table, r3(qb), r3(k_new), r3(v_new), r3(lf_new), r3(ga),
      *([cache_kt] * pages), *([cache_vt] * pages), *([cache_lft] * pages))
    return og.reshape(db, adim)


def _merge_out_kernel(x_ref, og_ref, yc_ref, g_ref, wmix_ref, wao_ref, wo_ref, fg_ref, out_ref, *, final):
    x = x_ref[0]
    d = x.shape[-1]
    h = _rmsnorm(x, g_ref[...]).astype(BF16)
    mix_c = _sigmoid(_dot(h, wmix_ref[:, 0:d]))
    mix_a = _sigmoid(_dot(h, wmix_ref[:, d:2 * d]))
    y_a = _dot(og_ref[0], wao_ref[...])
    m = mix_c * yc_ref[0] + mix_a * y_a
    y = x + _dot(m.astype(BF16), wo_ref[...])
    if final:
        y = _rmsnorm(y, fg_ref[...])
    out_ref[0] = y


def _merge_out(x, og, yc, layer, p, final_g, *, tq, final):
    bsz, seq, d = x.shape
    adim = og.shape[-1]
    assert seq % tq == 0
    tile = lambda w: pl.BlockSpec((1, tq, w), lambda b, t: (b, t, 0))
    in_specs = [tile(d), tile(adim), tile(d)] + [_const_spec(p[n].shape, layer) for n in
                                                  ("norm_g", "w_mix", "w_attn_out", "w_o")]
    in_specs.append(pl.BlockSpec(final_g.shape, lambda b, t: (0, 0)))
    return pl.pallas_call(
        functools.partial(_merge_out_kernel, final=final),
        grid=(bsz, seq // tq), in_specs=in_specs, out_specs=tile(d),
        out_shape=jax.ShapeDtypeStruct((bsz, seq, d), F32),
        compiler_params=pltpu.CompilerParams(dimension_semantics=("arbitrary", "arbitrary"),
                                             vmem_limit_bytes=VMEM_LIMIT),
        name="merge_out",
    )(x, og, yc, p["norm_g"], p["w_mix"], p["w_attn_out"], p["w_o"], final_g)


PROMPT_TILE = 512
ATTN_Q_TILE = 256
DECODE_PAGES = 16


def kernel(x_prompt, x_sample, cache_k, cache_v, cache_logf, state_conv, page_table, norm_g, w_in, b_f, w_dw,
           b_dw, ln_g, ln_b, w_conv_out, w_attn_out, w_o, final_g):
    depth, d = norm_g.shape
    bsz, seq, _ = x_prompt.shape
    db, dec_seq, _ = x_sample.shape
    assert dec_seq == 1
    cdim = w_dw.shape[-1]
    adim = N_HEADS * HEAD_DIM
    o_k = 3 * cdim + adim
    n_main = 3 * cdim + 4 * adim

    w_in_b = w_in.astype(BF16)
    w_f = w_in_b[:, :, n_main:n_main + N_HEADS]
    params = {
        "norm_g": norm_g.reshape(depth, 1, d),
        "w_main": w_in_b[:, :, :n_main],
        "w_kvt": jnp.swapaxes(w_in_b[:, :, o_k:o_k + 2 * adim], 1, 2),
        "w_f": jnp.pad(w_f, ((0, 0), (0, 0), (0, LANES - N_HEADS))),
        "w_ft": jnp.pad(jnp.swapaxes(w_f, 1, 2), ((0, 0), (0, BF16_ROWS - N_HEADS), (0, 0))),
        "w_mix": w_in_b[:, :, n_main + N_HEADS:],
        "b_f": jnp.pad(b_f, ((0, 0), (0, LANES - N_HEADS))).reshape(depth, 1, LANES),
        "b_ft": jnp.pad(b_f, ((0, 0), (0, BF16_ROWS - N_HEADS))).reshape(depth, BF16_ROWS, 1),
        "w_dw": w_dw,
        "b_dw": b_dw.reshape(depth, 1, cdim),
        "ln_g": ln_g.reshape(depth, 1, cdim),
        "ln_b": ln_b.reshape(depth, 1, cdim),
        "w_conv_out": w_conv_out.astype(BF16),
        "w_attn_out": w_attn_out.astype(BF16),
        "w_o": w_o.astype(BF16),
    }
    fg = final_g.reshape(1, d)
    n_pool = cache_k.shape[1]
    ckt = jnp.transpose(cache_k, (0, 1, 3, 4, 2)).reshape(depth, n_pool, adim, PAGE_SIZE)
    cvt = jnp.transpose(cache_v, (0, 1, 3, 4, 2)).reshape(depth, n_pool, adim, PAGE_SIZE)
    clft = jnp.transpose(cache_logf, (0, 1, 3, 2))
    state_t = jnp.transpose(state_conv, (0, 2, 1, 3))

    def heads_last(a_t, n):
        return jnp.transpose(a_t.reshape(n, N_HEADS, HEAD_DIM, a_t.shape[-1]), (0, 3, 1, 2))

    xp = x_prompt
    xs = x_sample.reshape(db, d)
    kp, vp, lp, cp, ksl, vsl, lsl, csl = ([] for _ in range(8))
    for layer in range(depth):
        final = layer == depth - 1
        kt32, vt32, lft, qb, ktb, vtb, ga, cum, cumt, yc, cst, kn2 = _proj_conv(xp, layer, params, tq=PROMPT_TILE)
        og = _fox_prompt(qb, ktb, vtb, cum, cumt, kn2, ga, tq=ATTN_Q_TILE)
        xp = _merge_out(xp, og, yc, layer, params, fg, tq=PROMPT_TILE, final=final)
        kp.append(heads_last(kt32, bsz))
        vp.append(heads_last(vt32, bsz))
        lp.append(jnp.transpose(lft, (0, 2, 1)))
        cp.append(cst)

        k_s, v_s, lf_s, q_s, ga_s, yc_s, u_s = _proj_conv_step(xs, state_t, layer, params)
        og_s = _fox_decode(q_s, k_s, v_s, lf_s, ga_s, ckt, cvt, clft, page_table, layer, pages=DECODE_PAGES)
        xs = _merge_out(xs[None], og_s[None], yc_s[None], layer, params, fg, tq=db, final=final)[0]
        ksl.append(k_s.reshape(db, 1, N_HEADS, HEAD_DIM))
        vsl.append(v_s.reshape(db, 1, N_HEADS, HEAD_DIM))
        lsl.append(lf_s.reshape(db, 1, N_HEADS))
        csl.append(jnp.transpose(jnp.concatenate([state_t[layer, 1:], u_s[None]], axis=0), (1, 0, 2)))

    return (xp, xs.reshape(db, 1, d), jnp.stack(kp), jnp.stack(vp), jnp.stack(lp), jnp.stack(cp),
            jnp.stack(ksl), jnp.stack(vsl), jnp.stack(lsl), jnp.stack(csl))
```

```python
import functools

import jax
import jax.numpy as jnp
from jax import lax
from jax.experimental import pallas as pl
from jax.experimental.pallas import tpu as pltpu

CONV_WIDTH = 31
N_HEADS = 8
HEAD_DIM = 64
PAGE_SIZE = 128
RMS_EPS = 1e-6
LN_EPS = 1e-5
ATTN_SCALE = HEAD_DIM ** -0.5
LOG2E = 1.4426950408889634

LANES = 128
SUBLANES = 8
BF16_ROWS = 16
HEADS_PER_BLOCK = LANES // HEAD_DIM
VMEM_LIMIT = 56 * 1024 * 1024

HIST = 32
CONV_ROWS = 64
NEG = -1e30
PRUNE_T = 90.0
KNORM_SLACK = (1.0 + 2.0 ** -7) ** 2

BF16 = jnp.bfloat16
F32 = jnp.float32


def _sigmoid(x):
    return 1.0 / (1.0 + jnp.exp(-x))


def _silu(x):
    return x * _sigmoid(x)


def _log_sigmoid(x):
    return jnp.minimum(x, 0.0) - jnp.log1p(jnp.exp(-jnp.abs(x)))


def _rmsnorm(x, g):
    return x * lax.rsqrt(jnp.mean(x * x, axis=-1, keepdims=True) + RMS_EPS) * g


def _layernorm(x, g, b):
    mu = jnp.mean(x, axis=-1, keepdims=True)
    xc = x - mu
    var = jnp.mean(xc * xc, axis=-1, keepdims=True)
    return xc * lax.rsqrt(var + LN_EPS) * g + b


def _dot(a, b):
    return jnp.dot(a, b, preferred_element_type=F32)


def _dot_nt(a, b):
    return lax.dot_general(a, b, (((1,), (1,)), ((), ())), preferred_element_type=F32)


def _split3(x):
    hi = x.astype(BF16)
    r = x - hi.astype(F32)
    mid = r.astype(BF16)
    lo = (r - mid.astype(F32)).astype(BF16)
    return hi, mid, lo


def _const_spec(shape, layer):
    nd = len(shape) - 1
    return pl.BlockSpec((None,) + tuple(shape[1:]), lambda *_: (layer,) + (0,) * nd)


def _proj_conv_kernel(*refs, tq, cdim, adim, nprev):
    x_ref = refs[0]
    kprev_ref, vprev_ref = refs[1:3] if nprev else (None, None)
    (g_ref, wmain_ref, wkvt_ref, wf_ref, bf_ref, wft_ref, bft_ref, wdw_ref, bdw_ref, lng_ref, lnb_ref, wco_ref,
     kt32_ref, vt32_ref, lft_ref, qb_ref, kb_ref, vtb_ref, ga_ref, cum_ref, cumt_ref, yc_ref, cst_ref, kn2_ref,
     h_scr, ubuf, c_scr, wb_scr, tri_scr, carry_scr, carryt_scr, kn2_scr) = refs[1 + 2 * bool(nprev):]
    b = pl.program_id(0)
    t = pl.program_id(1)
    nt = pl.num_programs(1)

    @pl.when((b == 0) & (t == 0))
    def _():
        row = lax.broadcasted_iota(jnp.int32, (tq, tq), 0)
        col = lax.broadcasted_iota(jnp.int32, (tq, tq), 1)
        tri_scr[...] = jnp.where(col <= row, 1.0, 0.0).astype(BF16)
        for j in range(CONV_WIDTH):
            wb_scr[j * SUBLANES:(j + 1) * SUBLANES, :] = jnp.broadcast_to(wdw_ref[j:j + 1, :], (SUBLANES, cdim))

    @pl.when(t == 0)
    def _():
        ubuf[0:HIST, :] = jnp.zeros((HIST, cdim), F32)
        carry_scr[...] = jnp.zeros_like(carry_scr)
        carryt_scr[...] = jnp.zeros_like(carryt_scr)
        kn2_scr[...] = jnp.zeros_like(kn2_scr)

    x = x_ref[0]
    h_scr[...] = _rmsnorm(x, g_ref[...]).astype(BF16)

    def proj(off, width):
        return _dot(h_scr[...], wmain_ref[:, off:off + width])

    o_q = 3 * cdim
    u = proj(0, cdim) * _sigmoid(proj(cdim, cdim))
    ubuf[HIST:HIST + tq, :] = u

    shift0 = HIST - (CONV_WIDTH - 1)
    groups = CONV_ROWS // SUBLANES
    sub = lax.broadcasted_iota(jnp.int32, (SUBLANES, LANES), 0)

    def conv_step(i, carry):
        r0 = pl.multiple_of(i * CONV_ROWS, CONV_ROWS)
        for cb in range(cdim // LANES):
            cols = slice(cb * LANES, (cb + 1) * LANES)
            win = [ubuf[pl.ds(r0 + SUBLANES * g, SUBLANES), cols] for g in range(groups + 4)]
            out = [jnp.broadcast_to(bdw_ref[:, cols], (SUBLANES, LANES)) for _ in range(groups)]
            for res in range(SUBLANES):
                taps = [j for j in range(CONV_WIDTH) if (shift0 + j) % SUBLANES == res]
                wts = [wb_scr[j * SUBLANES:(j + 1) * SUBLANES, cols] for j in taps]
                part = []
                for g in range(groups + (res > 0)):
                    acc = None
                    for j, w in zip(taps, wts):
                        term = w * win[g + (shift0 + j) // SUBLANES]
                        acc = term if acc is None else acc + term
                    part.append(acc)
                for g in range(groups):
                    if res == 0:
                        out[g] = out[g] + part[g]
                    else:
                        sel = jnp.where(sub >= res, part[g], part[g + 1])
                        out[g] = out[g] + pltpu.roll(sel, SUBLANES - res, axis=0)
            for g in range(groups):
                c_scr[pl.ds(r0 + SUBLANES * g, SUBLANES), cols] = out[g]
        return carry

    lax.fori_loop(0, tq // CONV_ROWS, conv_step, 0)

    @pl.when(t == nt - 1)
    def _():
        cst_ref[0] = ubuf[HIST + tq - (CONV_WIDTH - 1):HIST + tq, :]

    ubuf[0:HIST, :] = ubuf[tq:tq + HIST, :]

    cact = _silu(_layernorm(c_scr[...], lng_ref[...], lnb_ref[...])) * _silu(proj(2 * cdim, cdim))
    yc_ref[0] = _dot(cact.astype(BF16), wco_ref[...])

    qb_ref[0] = (proj(o_q, adim) * (ATTN_SCALE * LOG2E)).astype(BF16)
    ga_ref[0] = _silu(proj(o_q + 3 * adim, adim))
    kt = _dot_nt(wkvt_ref[0:adim, :], h_scr[...])
    kt32_ref[nprev, 0] = kt
    if nprev:
        kt32_ref[0:nprev] = kprev_ref[...]
        vt32_ref[0:nprev] = vprev_ref[...]
    kb_ref[0] = proj(o_q + adim, adim).astype(BF16)
    kf = kt.astype(BF16).astype(F32)
    ksq = kf * kf * KNORM_SLACK
    hrow = lax.broadcasted_iota(jnp.int32, (N_HEADS, LANES), 0)
    kn2 = kn2_scr[...]
    for hd in range(N_HEADS):
        col = jnp.sum(ksq[hd * HEAD_DIM:(hd + 1) * HEAD_DIM, :], axis=0, keepdims=True)
        top = jnp.max(col, axis=1, keepdims=True)
        kn2 = jnp.where(hrow == hd, jnp.maximum(kn2, top), kn2)
    kn2_scr[...] = kn2
    kn2_ref[0, 0] = kn2
    vt = _dot_nt(wkvt_ref[adim:2 * adim, :], h_scr[...])
    vt32_ref[nprev, 0] = vt
    vtb_ref[0, 0] = vt.astype(BF16)

    tri = tri_scr[...]
    lft = _log_sigmoid(_dot_nt(wft_ref[...], h_scr[...]) + bft_ref[...])[:N_HEADS, :]
    lft_ref[0] = lft
    cst = carryt_scr[...]
    for part in _split3(lft):
        cst = cst + _dot_nt(part, tri)
    carryt_scr[...] = cst[:, tq - 1:tq]
    cumt_ref[0] = cst
    lane = lax.broadcasted_iota(jnp.int32, (tq, LANES), 1)
    lf = _log_sigmoid(_dot(h_scr[...], wf_ref[...]) + bf_ref[...])
    lf = jnp.where(lane < N_HEADS, lf, 0.0)
    cs = carry_scr[...]
    for part in _split3(lf):
        cs = cs + _dot(tri, part)
    carry_scr[...] = cs[tq - 1:tq, :]
    cum_ref[0] = cs[:, :N_HEADS]


def _proj_conv(x, layer, p, prev_kv, *, tq):
    bsz, seq, d = x.shape
    cdim = p["w_dw"].shape[-1]
    adim = N_HEADS * HEAD_DIM
    nt = seq // tq
    nprev = 0 if prev_kv is None else prev_kv[0].shape[0]
    assert seq % tq == 0 and tq % CONV_ROWS == 0 and tq >= HIST
    tile = lambda w: pl.BlockSpec((1, tq, w), lambda b, t: (b, t, 0))
    tile_t = lambda r: pl.BlockSpec((1, r, tq), lambda b, t: (b, 0, t))
    chunk_t = lambda r: pl.BlockSpec((1, 1, r, tq), lambda b, t: (b, t, 0, 0))
    stack_t = lambda n: pl.BlockSpec((n, 1, adim, tq), lambda b, t: (0, b, 0, t))
    out_shape = (
        jax.ShapeDtypeStruct((nprev + 1, bsz, adim, seq), F32),
        jax.ShapeDtypeStruct((nprev + 1, bsz, adim, seq), F32),
        jax.ShapeDtypeStruct((bsz, N_HEADS, seq), F32),
        jax.ShapeDtypeStruct((bsz, seq, adim), BF16),
        jax.ShapeDtypeStruct((bsz, seq, adim), BF16),
        jax.ShapeDtypeStruct((bsz, nt, adim, tq), BF16),
        jax.ShapeDtypeStruct((bsz, seq, adim), F32),
        jax.ShapeDtypeStruct((bsz, seq, N_HEADS), F32),
        jax.ShapeDtypeStruct((bsz, N_HEADS, seq), F32),
        jax.ShapeDtypeStruct((bsz, seq, d), F32),
        jax.ShapeDtypeStruct((bsz, CONV_WIDTH - 1, cdim), F32),
        jax.ShapeDtypeStruct((bsz, nt, N_HEADS, LANES), F32),
    )
    out_specs = (
        stack_t(nprev + 1), stack_t(nprev + 1), tile_t(N_HEADS), tile(adim), tile(adim), chunk_t(adim), tile(adim),
        tile(N_HEADS), tile_t(N_HEADS), tile(d),
        pl.BlockSpec((1, CONV_WIDTH - 1, cdim), lambda b, t: (b, 0, 0)),
        pl.BlockSpec((1, 1, N_HEADS, LANES), lambda b, t: (b, t, 0, 0)),
    )
    names = ("norm_g", "w_main", "w_kvt", "w_f", "b_f", "w_ft", "b_ft", "w_dw", "b_dw", "ln_g", "ln_b",
             "w_conv_out")
    prev = () if prev_kv is None else tuple(prev_kv)
    in_specs = [tile(d)] + [stack_t(nprev)] * len(prev) + [_const_spec(p[n].shape, layer) for n in names]
    scratch = [
        pltpu.VMEM((tq, d), BF16),
        pltpu.VMEM((HIST + tq, cdim), F32),
        pltpu.VMEM((tq, cdim), F32),
        pltpu.VMEM((CONV_WIDTH * SUBLANES, cdim), F32),
        pltpu.VMEM((tq, tq), BF16),
        pltpu.VMEM((1, LANES), F32),
        pltpu.VMEM((N_HEADS, 1), F32),
        pltpu.VMEM((N_HEADS, LANES), F32),
    ]
    return pl.pallas_call(
        functools.partial(_proj_conv_kernel, tq=tq, cdim=cdim, adim=adim, nprev=nprev),
        grid=(bsz, nt), in_specs=in_specs, out_specs=out_specs, out_shape=out_shape, scratch_shapes=scratch,
        compiler_params=pltpu.CompilerParams(dimension_semantics=("arbitrary", "arbitrary"),
                                             vmem_limit_bytes=VMEM_LIMIT),
        name="proj_conv",
    )(x, *prev, *(p[n] for n in names))


def _proj_conv_step_kernel(x_ref, st_ref, g_ref, wmain_ref, wf_ref, bf_ref, wdw_ref, bdw_ref, lng_ref, lnb_ref,
                           wco_ref, k32_ref, v32_ref, logf_ref, qb_ref, ga_ref, yc_ref, u_ref, *, cdim, adim):
    h = _rmsnorm(x_ref[...], g_ref[...]).astype(BF16)

    def proj(off, width):
        return _dot(h, wmain_ref[:, off:off + width])

    o_q = 3 * cdim
    u = proj(0, cdim) * _sigmoid(proj(cdim, cdim))
    u_ref[...] = u
    c = wdw_ref[CONV_WIDTH - 1:CONV_WIDTH, :] * u + bdw_ref[...]
    for j in range(CONV_WIDTH - 1):
        c = c + wdw_ref[j:j + 1, :] * st_ref[j]
    cact = _silu(_layernorm(c, lng_ref[...], lnb_ref[...])) * _silu(proj(2 * cdim, cdim))
    yc_ref[...] = _dot(cact.astype(BF16), wco_ref[...])
    qb_ref[...] = (proj(o_q, adim) * ATTN_SCALE).astype(BF16)
    k32_ref[...] = proj(o_q + adim, adim)
    v32_ref[...] = proj(o_q + 2 * adim, adim)
    ga_ref[...] = _silu(proj(o_q + 3 * adim, adim))
    lf = _log_sigmoid(_dot(h, wf_ref[...]) + bf_ref[...])
    logf_ref[...] = lf[:, :N_HEADS]


def _proj_conv_step(x, state_t, layer, p):
    rows, d = x.shape
    cdim = p["w_dw"].shape[-1]
    adim = N_HEADS * HEAD_DIM
    full = lambda shape: pl.BlockSpec(shape, lambda i: (0,) * len(shape))
    out_shape = (
        jax.ShapeDtypeStruct((rows, adim), F32), jax.ShapeDtypeStruct((rows, adim), F32),
        jax.ShapeDtypeStruct((rows, N_HEADS), F32), jax.ShapeDtypeStruct((rows, adim), BF16),
        jax.ShapeDtypeStruct((rows, adim), F32), jax.ShapeDtypeStruct((rows, d), F32),
        jax.ShapeDtypeStruct((rows, cdim), F32),
    )
    names = ("norm_g", "w_main", "w_f", "b_f", "w_dw", "b_dw", "ln_g", "ln_b", "w_conv_out")
    in_specs = [full(x.shape), _const_spec(state_t.shape, layer)] + [_const_spec(p[n].shape, layer) for n in names]
    return pl.pallas_call(
        functools.partial(_proj_conv_step_kernel, cdim=cdim, adim=adim),
        grid=(1,), in_specs=in_specs, out_specs=tuple(full(s.shape) for s in out_shape), out_shape=out_shape,
        compiler_params=pltpu.CompilerParams(dimension_semantics=("arbitrary",), vmem_limit_bytes=VMEM_LIMIT),
        name="proj_conv_step",
    )(x, state_t, *(p[n] for n in names))


def _fox_prompt_kernel(q_ref, k_ref, vt_ref, cum_ref, cumt_ref, cumend_ref, kn2_ref, ga_ref, og_ref, ckb_scr,
                       s_scr, smax_scr, *, tq, tk):
    hp = pl.program_id(1)
    qi = pl.program_id(2)
    q0 = qi * tq
    n_full = lax.div(q0, tk)
    nk = cumend_ref.shape[-1]

    @pl.when(qi == 0)
    def _():
        hlane = lax.broadcasted_iota(jnp.int32, (1, N_HEADS), 1)

        def fill(c, carry):
            r0 = pl.multiple_of(c * tk, tk)
            blk = cum_ref[0, pl.ds(r0, tk), :]
            for e in range(HEADS_PER_BLOCK):
                col = jnp.sum(jnp.where(hlane == HEADS_PER_BLOCK * hp + e, blk, 0.0), axis=1, keepdims=True)
                ckb_scr[e, pl.ds(r0, tk), :] = jnp.broadcast_to(col * LOG2E, (tk, LANES))
            return carry

        lax.fori_loop(0, nk, fill, 0)

    q2 = q_ref[0]
    lane = lax.broadcasted_iota(jnp.int32, (1, LANES), 1)
    lower = lane < HEAD_DIM
    zero = jnp.zeros_like(q2)
    qs = (jnp.where(lower, q2, zero), jnp.where(lower, zero, q2))
    cqs = tuple(cumt_ref[0, pl.ds(HEADS_PER_BLOCK * hp + e, 1), :] * LOG2E
                for e in range(HEADS_PER_BLOCK))

    def scores(ki, causal=False):
        k0 = pl.multiple_of(ki * tk, tk)
        k2 = k_ref[0, pl.ds(k0, tk), :]
        out = []
        for e in range(HEADS_PER_BLOCK):
            ck = ckb_scr[e, pl.ds(k0, tk), :]
            s = _dot_nt(k2, qs[e]) - jnp.concatenate([ck] * (tq // LANES), axis=1)
            if causal:
                ahead = (lax.broadcasted_iota(jnp.int32, (tk, tq), 0)
                         - lax.broadcasted_iota(jnp.int32, (tk, tq), 1))
                s = jnp.where(ahead <= q0 - k0, s, NEG)
            out.append((s, jnp.max(s, axis=0, keepdims=True)))
        return out

    def update(ki, sc, state, valid=None):
        vt = vt_ref[0, ki]
        new = []
        for e in range(HEADS_PER_BLOCK):
            m_old, l_old, acc_old = state[3 * e:3 * e + 3]
            s, smax = sc[e]
            m_new = jnp.maximum(m_old, smax + cqs[e])
            if valid is not None:
                m_new = jnp.where(valid, m_new, m_old)
            p = jnp.exp2(s - (m_new - cqs[e]))
            alpha = jnp.exp2(m_old - m_new)
            l_new = alpha * l_old + jnp.sum(p, axis=0, keepdims=True)
            acc_new = alpha * acc_old + _dot(vt[e * HEAD_DIM:(e + 1) * HEAD_DIM, :], p.astype(BF16))
            if valid is not None:
                l_new = jnp.where(valid, l_new, l_old)
                acc_new = jnp.where(valid, acc_new, acc_old)
            new += [m_new, l_new, acc_new]
        return tuple(new)

    init = (jnp.full((1, tq), NEG, F32), jnp.zeros((1, tq), F32), jnp.zeros((HEAD_DIM, tq), F32)) * HEADS_PER_BLOCK

    def produce(buf, ki):
        for e, (s, smax) in enumerate(scores(jnp.maximum(ki, 0))):
            s_scr[buf, e] = s
            smax_scr[buf, e] = smax

    def consume(buf, ki, st):
        sc = [(s_scr[buf, e], smax_scr[buf, e]) for e in range(HEADS_PER_BLOCK)]
        return update(jnp.maximum(ki, 0), sc, st, valid=ki >= 0)

    diag = scores(n_full, causal=True)
    produce(0, n_full - 1)
    produce(1, n_full - 2)
    state = update(n_full, diag, init)

    qf = q2.astype(F32)
    qsq = qf * qf
    kidx = lax.broadcasted_iota(jnp.int32, (1, nk), 1)
    first_live = []
    for e in range(HEADS_PER_BLOCK):
        head = HEADS_PER_BLOCK * hp + e
        own_lanes = lower if e == 0 else jnp.logical_not(lower)
        qn2 = jnp.max(jnp.sum(jnp.where(own_lanes, qsq, 0.0), axis=1, keepdims=True), axis=0, keepdims=True)
        kn2 = kn2_ref[0, n_full, pl.ds(head, 1), :][:, 0:1]
        slack = jnp.max(cqs[e] - state[3 * e], axis=1, keepdims=True)
        reach = jnp.sqrt(qn2 * kn2) + slack + PRUNE_T * LOG2E
        live = (cumend_ref[0, pl.ds(head, 1), :] * LOG2E <= reach) | (kidx >= n_full)
        first_live.append(jnp.min(jnp.where(live, kidx, nk).astype(F32)))
    ki_start = jnp.minimum(first_live[0], first_live[1]).astype(jnp.int32)

    npairs = lax.div(n_full - ki_start + 1, 2)

    def sweep(j, st):
        top = n_full - 1 - 2 * j
        st = consume(0, top, st)
        produce(0, top - 2)
        st = consume(1, top - 1, st)
        produce(1, top - 3)
        return st

    state = lax.fori_loop(0, npairs - 1, sweep, state)

    def last_pair(st):
        top = n_full + 1 - 2 * npairs
        return consume(1, top - 1, consume(0, top, st))

    state = lax.cond(npairs >= 1, last_pair, lambda st: st, state)

    o_t = jnp.concatenate([state[2] / state[1], state[5] / state[4]], axis=0)
    og_ref[0] = (o_t.T * ga_ref[0]).astype(BF16)


def _fox_prompt(qb, kb, vtb, cum, cumt, kn2, ga, *, tq):
    bsz, seq, adim = qb.shape
    nk, tk = vtb.shape[1], vtb.shape[3]
    assert seq % tq == 0 and tk % tq == 0 and nk * tk == seq
    nhp = adim // LANES
    cumend = cumt[:, :, tk - 1::tk]
    qtile = pl.BlockSpec((1, tq, LANES), lambda b, hp, qi: (b, qi, hp))
    return pl.pallas_call(
        functools.partial(_fox_prompt_kernel, tq=tq, tk=tk),
        grid=(bsz, nhp, seq // tq),
        in_specs=[qtile,
                  pl.BlockSpec((1, seq, LANES), lambda b, hp, qi: (b, 0, hp)),
                  pl.BlockSpec((1, nk, LANES, tk), lambda b, hp, qi: (b, 0, hp, 0)),
                  pl.BlockSpec((1, seq, N_HEADS), lambda b, hp, qi: (b, 0, 0)),
                  pl.BlockSpec((1, N_HEADS, tq), lambda b, hp, qi: (b, 0, qi)),
                  pl.BlockSpec((1, N_HEADS, nk), lambda b, hp, qi: (b, 0, 0)),
                  pl.BlockSpec((1, nk, N_HEADS, LANES), lambda b, hp, qi: (b, 0, 0, 0)),
                  qtile],
        out_specs=qtile,
        out_shape=jax.ShapeDtypeStruct((bsz, seq, adim), BF16),
        scratch_shapes=[pltpu.VMEM((HEADS_PER_BLOCK, seq, LANES), F32),
                        pltpu.VMEM((2, HEADS_PER_BLOCK, tk, tq), F32),
                        pltpu.VMEM((2, HEADS_PER_BLOCK, 1, tq), F32)],
        compiler_params=pltpu.CompilerParams(dimension_semantics=("arbitrary",) * 3, vmem_limit_bytes=VMEM_LIMIT),
        name="fox_prompt",
    )(qb, kb, vtb, cum, cumt, cumend, kn2, ga)


def _own_lanes(adim):
    sub = lax.broadcasted_iota(jnp.int32, (N_HEADS, adim), 0)
    lane_head = lax.broadcasted_iota(jnp.int32, (N_HEADS, adim), 1) // HEAD_DIM
    return sub == lane_head


def _fox_decode_scores_kernel(pt_ref, q_ref, kn_ref, lfn_ref, *rest, pages):
    k_refs = rest[:pages]
    lf_refs = rest[pages:2 * pages]
    pn_ref, pself_ref, live_ref, s_scr, m_scr, self_scr, carry_scr = rest[2 * pages:]
    del pt_ref
    j = pl.program_id(1)
    steps = pl.num_programs(1)
    adim = N_HEADS * HEAD_DIM

    qbd32 = jnp.where(_own_lanes(adim), jnp.broadcast_to(q_ref[0].astype(F32), (N_HEADS, adim)), 0.0)
    qbd = qbd32.astype(BF16)

    @pl.when(j == 0)
    def _():
        kn = kn_ref[0].astype(BF16).astype(F32)
        s_self = jnp.sum(qbd32 * kn, axis=1, keepdims=True)
        self_scr[...] = s_self
        m_scr[...] = s_self
        hs = lax.broadcasted_iota(jnp.int32, (N_HEADS, N_HEADS), 0)
        hl = lax.broadcasted_iota(jnp.int32, (N_HEADS, N_HEADS), 1)
        carry_scr[...] = jnp.sum(jnp.where(hs == hl, jnp.broadcast_to(lfn_ref[0], (N_HEADS, N_HEADS)), 0.0),
                                 axis=1, keepdims=True)

    lf_all = jnp.concatenate([r[...] for r in lf_refs], axis=0)
    jr = lax.broadcasted_iota(jnp.int32, (PAGE_SIZE, PAGE_SIZE), 0)
    sc = lax.broadcasted_iota(jnp.int32, (PAGE_SIZE, PAGE_SIZE), 1)
    later = jnp.where(jr > sc, 1.0, 0.0).astype(BF16)
    within = None
    for part in _split3(lf_all):
        w = _dot(part, later)
        within = w if within is None else within + w

    run = carry_scr[...]
    s_parts = [None] * pages
    for i in reversed(range(pages)):
        bias = within[i * N_HEADS:(i + 1) * N_HEADS, :] + run
        s_parts[i] = _dot(qbd, k_refs[i][...].astype(BF16)) + bias
        run = run + jnp.sum(lf_refs[i][...], axis=1, keepdims=True)
    carry_scr[...] = run

    s = jnp.concatenate(s_parts, axis=1)
    s_scr[steps - 1 - j] = s
    m_new = jnp.maximum(m_scr[...], jnp.max(s, axis=1, keepdims=True))
    m_scr[...] = m_new

    lane = lax.broadcasted_iota(jnp.int32, (1, LANES), 1)
    flags = jnp.zeros((1, LANES), F32)
    for i in range(pages):
        top = jnp.max(s_parts[i] - m_new, axis=1, keepdims=True)
        alive = jnp.max(jnp.exp(top), axis=0, keepdims=True)
        flags = jnp.where(lane == i, alive, flags)
    live_ref[0, 0] = flags

    @pl.when(j == steps - 1)
    def _():
        m = m_scr[...]
        p_self = jnp.exp(self_scr[...] - m)
        l = p_self
        for jj in range(s_scr.shape[0]):
            p = jnp.exp(s_scr[jj] - m)
            s_scr[jj] = p
            l = l + jnp.sum(p, axis=1, keepdims=True)
        inv = 1.0 / l
        for jj in range(s_scr.shape[0]):
            pn_ref[0, jj] = s_scr[jj] * inv
        pself_ref[0] = jnp.broadcast_to(p_self * inv, (N_HEADS, LANES))


def _fox_decode_pv_kernel(pt_ref, any_ref, pn_ref, pself_ref, vn_ref, ga_ref, *rest, pages):
    v_refs = rest[:pages]
    og_ref, acc_scr = rest[pages:]
    del pt_ref
    b = pl.program_id(0)
    j = pl.program_id(1)
    adim = N_HEADS * HEAD_DIM

    @pl.when(j == 0)
    def _():
        acc_scr[...] = jnp.zeros_like(acc_scr)

    @pl.when(any_ref[b, j] > 0)
    def _():
        pn = pn_ref[0, 0]
        pv = jnp.zeros((N_HEADS, adim), F32)
        for i in range(pages):
            pv = pv + _dot_nt(pn[:, i * PAGE_SIZE:(i + 1) * PAGE_SIZE].astype(BF16), v_refs[i][...].astype(BF16))
        acc_scr[...] = acc_scr[...] + pv

    @pl.when(j == pl.num_programs(1) - 1)
    def _():
        acc = acc_scr[...] + pself_ref[0][:, 0:1] * vn_ref[0].astype(BF16).astype(F32)
        o = jnp.sum(jnp.where(_own_lanes(adim), acc, 0.0), axis=0, keepdims=True)
        og_ref[0] = (o * ga_ref[0]).astype(BF16)


def _fox_decode(qb, k_new, v_new, lf_new, ga, cache_kt, cache_vt, cache_lft, page_table, layer, *, pages):
    db, adim = qb.shape
    n_pages = page_table.shape[1]
    assert n_pages % pages == 0 and pages <= LANES
    steps = n_pages // pages
    span = pages * PAGE_SIZE
    r3 = lambda a: a.reshape(db, 1, a.shape[-1])
    params = pltpu.CompilerParams(dimension_semantics=("arbitrary", "arbitrary"), vmem_limit_bytes=VMEM_LIMIT)

    row = lambda w: pl.BlockSpec((1, 1, w), lambda b, j, pt: (b, 0, 0))

    def page_spec(i, rows):
        return pl.BlockSpec((None, None, rows, PAGE_SIZE),
                            lambda b, j, pt: (layer, pt[b, (steps - 1 - j) * pages + i], 0, 0))

    pn, pself, live = pl.pallas_call(
        functools.partial(_fox_decode_scores_kernel, pages=pages),
        grid_spec=pltpu.PrefetchScalarGridSpec(
            num_scalar_prefetch=1, grid=(db, steps),
            in_specs=([row(adim), row(adim), row(N_HEADS)] + [page_spec(i, adim) for i in range(pages)]
                      + [page_spec(i, N_HEADS) for i in range(pages)]),
            out_specs=(pl.BlockSpec((1, steps, N_HEADS, span), lambda b, j, pt: (b, 0, 0, 0)),
                       pl.BlockSpec((1, N_HEADS, LANES), lambda b, j, pt: (b, 0, 0)),
                       pl.BlockSpec((1, 1, 1, LANES), lambda b, j, pt: (b, steps - 1 - j, 0, 0))),
            scratch_shapes=[pltpu.VMEM((steps, N_HEADS, span), F32), pltpu.VMEM((N_HEADS, 1), F32),
                            pltpu.VMEM((N_HEADS, 1), F32), pltpu.VMEM((N_HEADS, 1), F32)]),
        out_shape=(jax.ShapeDtypeStruct((db, steps, N_HEADS, span), F32),
                   jax.ShapeDtypeStruct((db, N_HEADS, LANES), F32),
                   jax.ShapeDtypeStruct((db, steps, 1, LANES), F32)),
        compiler_params=params, name="fox_decode_scores",
    )(page_table, r3(qb), r3(k_new), r3(lf_new), *([cache_kt] * pages), *([cache_lft] * pages))

    alive = (live[:, :, 0, :pages] > 0.0).reshape(db * steps, pages)
    order = jnp.arange(db * steps, dtype=jnp.int32)[:, None]
    last = jnp.maximum(lax.cummax(jnp.where(alive, order, -1), axis=0), 0)
    page_ids = jnp.take_along_axis(page_table.reshape(db * steps, pages), last, axis=0).reshape(db, n_pages)
    any_alive = jnp.any(alive, axis=1).reshape(db, steps).astype(jnp.int32)

    row2 = lambda w: pl.BlockSpec((1, 1, w), lambda b, j, pt, al: (b, 0, 0))

    def value_spec(i):
        return pl.BlockSpec((None, None, adim, PAGE_SIZE), lambda b, j, pt, al: (layer, pt[b, j * pages + i], 0, 0))

    og = pl.pallas_call(
        functools.partial(_fox_decode_pv_kernel, pages=pages),
        grid_spec=pltpu.PrefetchScalarGridSpec(
            num_scalar_prefetch=2, grid=(db, steps),
            in_specs=([pl.BlockSpec((1, 1, N_HEADS, span), lambda b, j, pt, al: (b, j, 0, 0)),
                       pl.BlockSpec((1, N_HEADS, LANES), lambda b, j, pt, al: (b, 0, 0)),
                       row2(adim), row2(adim)] + [value_spec(i) for i in range(pages)]),
            out_specs=row2(adim),
            scratch_shapes=[pltpu.VMEM((N_HEADS, adim), F32)]),
        out_shape=jax.ShapeDtypeStruct((db, 1, adim), BF16),
        compiler_params=params, name="fox_decode_pv",
    )(page_ids, any_alive, pn, pself, r3(v_new), r3(ga), *([cache_vt] * pages))
    return og.reshape(db, adim)


def _merge_out_kernel(x_ref, og_ref, yc_ref, g_ref, wmix_ref, wao_ref, wo_ref, fg_ref, out_ref, *, final):
    x = x_ref[0]
    d = x.shape[-1]
    h = _rmsnorm(x, g_ref[...]).astype(BF16)
    mix_c = _sigmoid(_dot(h, wmix_ref[:, 0:d]))
    mix_a = _sigmoid(_dot(h, wmix_ref[:, d:2 * d]))
    y_a = _dot(og_ref[0], wao_ref[...])
    m = mix_c * yc_ref[0] + mix_a * y_a
    y = x + _dot(m.astype(BF16), wo_ref[...])
    if final:
        y = _rmsnorm(y, fg_ref[...])
    out_ref[0] = y


def _merge_out(x, og, yc, layer, p, final_g, *, tq, final):
    bsz, seq, d = x.shape
    adim = og.shape[-1]
    assert seq % tq == 0
    tile = lambda w: pl.BlockSpec((1, tq, w), lambda b, t: (b, t, 0))
    in_specs = [tile(d), tile(adim), tile(d)] + [_const_spec(p[n].shape, layer) for n in
                                                  ("norm_g", "w_mix", "w_attn_out", "w_o")]
    in_specs.append(pl.BlockSpec(final_g.shape, lambda b, t: (0, 0)))
    return pl.pallas_call(
        functools.partial(_merge_out_kernel, final=final),
        grid=(bsz, seq // tq), in_specs=in_specs, out_specs=tile(d),
        out_shape=jax.ShapeDtypeStruct((bsz, seq, d), F32),
        compiler_params=pltpu.CompilerParams(dimension_semantics=("arbitrary", "arbitrary"),
                                             vmem_limit_bytes=VMEM_LIMIT),
        name="merge_out",
    )(x, og, yc, p["norm_g"], p["w_mix"], p["w_attn_out"], p["w_o"], final_g)


PROMPT_TILE = 512
ATTN_Q_TILE = 256
DECODE_PAGES = 16


def kernel(x_prompt, x_sample, cache_k, cache_v, cache_logf, state_conv, page_table, norm_g, w_in, b_f, w_dw,
           b_dw, ln_g, ln_b, w_conv_out, w_attn_out, w_o, final_g):
    depth, d = norm_g.shape
    bsz, seq, _ = x_prompt.shape
    db, dec_seq, _ = x_sample.shape
    assert dec_seq == 1
    cdim = w_dw.shape[-1]
    adim = N_HEADS * HEAD_DIM
    o_k = 3 * cdim + adim
    n_main = 3 * cdim + 4 * adim

    w_in_b = w_in.astype(BF16)
    w_f = w_in_b[:, :, n_main:n_main + N_HEADS]
    params = {
        "norm_g": norm_g.reshape(depth, 1, d),
        "w_main": w_in_b[:, :, :n_main],
        "w_kvt": jnp.swapaxes(w_in_b[:, :, o_k:o_k + 2 * adim], 1, 2),
        "w_f": jnp.pad(w_f, ((0, 0), (0, 0), (0, LANES - N_HEADS))),
        "w_ft": jnp.pad(jnp.swapaxes(w_f, 1, 2), ((0, 0), (0, BF16_ROWS - N_HEADS), (0, 0))),
        "w_mix": w_in_b[:, :, n_main + N_HEADS:],
        "b_f": jnp.pad(b_f, ((0, 0), (0, LANES - N_HEADS))).reshape(depth, 1, LANES),
        "b_ft": jnp.pad(b_f, ((0, 0), (0, BF16_ROWS - N_HEADS))).reshape(depth, BF16_ROWS, 1),
        "w_dw": w_dw,
        "b_dw": b_dw.reshape(depth, 1, cdim),
        "ln_g": ln_g.reshape(depth, 1, cdim),
        "ln_b": ln_b.reshape(depth, 1, cdim),
        "w_conv_out": w_conv_out.astype(BF16),
        "w_attn_out": w_attn_out.astype(BF16),
        "w_o": w_o.astype(BF16),
    }
    fg = final_g.reshape(1, d)
    n_pool = cache_k.shape[1]
    ckt = jnp.transpose(cache_k, (0, 1, 3, 4, 2)).reshape(depth, n_pool, adim, PAGE_SIZE)
    cvt = jnp.transpose(cache_v, (0, 1, 3, 4, 2)).reshape(depth, n_pool, adim, PAGE_SIZE)
    clft = jnp.transpose(cache_logf, (0, 1, 3, 2))
    state_t = jnp.transpose(state_conv, (0, 2, 1, 3))

    def heads_last(a_t):
        return jnp.transpose(a_t.reshape(depth, bsz, N_HEADS, HEAD_DIM, seq), (0, 1, 4, 2, 3))

    xp = x_prompt
    xs = x_sample.reshape(db, d)
    lp, cp, ksl, vsl, lsl, csl = ([] for _ in range(6))
    kv_stack = None
    for layer in range(depth):
        final = layer == depth - 1
        kt32, vt32, lft, qb, kb, vtb, ga, cum, cumt, yc, cst, kn2 = _proj_conv(xp, layer, params, kv_stack,
                                                                                tq=PROMPT_TILE)
        kv_stack = (kt32, vt32)
        og = _fox_prompt(qb, kb, vtb, cum, cumt, kn2, ga, tq=ATTN_Q_TILE)
        xp = _merge_out(xp, og, yc, layer, params, fg, tq=PROMPT_TILE, final=final)
        lp.append(jnp.transpose(lft, (0, 2, 1)))
        cp.append(cst)

        k_s, v_s, lf_s, q_s, ga_s, yc_s, u_s = _proj_conv_step(xs, state_t, layer, params)
        og_s = _fox_decode(q_s, k_s, v_s, lf_s, ga_s, ckt, cvt, clft, page_table, layer, pages=DECODE_PAGES)
        xs = _merge_out(xs[None], og_s[None], yc_s[None], layer, params, fg, tq=db, final=final)[0]
        ksl.append(k_s.reshape(db, 1, N_HEADS, HEAD_DIM))
        vsl.append(v_s.reshape(db, 1, N_HEADS, HEAD_DIM))
        lsl.append(lf_s.reshape(db, 1, N_HEADS))
        csl.append(jnp.transpose(jnp.concatenate([state_t[layer, 1:], u_s[None]], axis=0), (1, 0, 2)))

    return (xp, xs.reshape(db, 1, d), heads_last(kv_stack[0]), heads_last(kv_stack[1]), jnp.stack(lp), jnp.stack(cp),
            jnp.stack(ksl), jnp.stack(vsl), jnp.stack(lsl), jnp.stack(csl))
```

```python
import functools

import jax
import jax.numpy as jnp
from jax import lax
from jax.experimental import pallas as pl
from jax.experimental.pallas import tpu as pltpu

CONV_WIDTH = 31
N_HEADS = 8
HEAD_DIM = 64
PAGE_SIZE = 128
RMS_EPS = 1e-6
LN_EPS = 1e-5
ATTN_SCALE = HEAD_DIM ** -0.5
LOG2E = 1.4426950408889634

LANES = 128
SUBLANES = 8
BF16_ROWS = 16
HEADS_PER_BLOCK = LANES // HEAD_DIM
VMEM_LIMIT = 56 * 1024 * 1024

HIST = 32
CONV_ROWS = 64
NEG = -1e30
PRUNE_T = 90.0
KNORM_SLACK = (1.0 + 2.0 ** -7) ** 2

BF16 = jnp.bfloat16
F32 = jnp.float32


def _sigmoid(x):
    return 0.5 * jnp.tanh(0.5 * x) + 0.5


def _silu(x):
    return x * _sigmoid(x)


def _log_sigmoid(x):
    return jnp.minimum(x, 0.0) - jnp.log1p(jnp.exp(-jnp.abs(x)))


def _rmsnorm(x, g):
    return x * lax.rsqrt(jnp.mean(x * x, axis=-1, keepdims=True) + RMS_EPS) * g


def _layernorm(x, g, b):
    mu = jnp.mean(x, axis=-1, keepdims=True)
    xc = x - mu
    var = jnp.mean(xc * xc, axis=-1, keepdims=True)
    return xc * lax.rsqrt(var + LN_EPS) * g + b


def _dot(a, b):
    return jnp.dot(a, b, preferred_element_type=F32)


def _dot_nt(a, b):
    return lax.dot_general(a, b, (((1,), (1,)), ((), ())), preferred_element_type=F32)


def _split3(x):
    hi = x.astype(BF16)
    r = x - hi.astype(F32)
    mid = r.astype(BF16)
    lo = (r - mid.astype(F32)).astype(BF16)
    return hi, mid, lo


def _const_spec(shape, layer):
    nd = len(shape) - 1
    return pl.BlockSpec((None,) + tuple(shape[1:]), lambda *_: (layer,) + (0,) * nd)


def _proj_conv_kernel(*refs, tq, cdim, adim, nprev):
    x_ref = refs[0]
    kprev_ref, vprev_ref = refs[1:3] if nprev else (None, None)
    (g_ref, wmain_ref, wkvt_ref, wf_ref, bf_ref, wft_ref, bft_ref, wdw_ref, bdw_ref, lng_ref, lnb_ref, wco_ref,
     kt32_ref, vt32_ref, lft_ref, qb_ref, kb_ref, vtb_ref, ga_ref, cum_ref, cumt_ref, yc_ref, cst_ref, kn2_ref,
     h_scr, ubuf, c_scr, wb_scr, tri_scr, carry_scr, carryt_scr, kn2_scr) = refs[1 + 2 * bool(nprev):]
    b = pl.program_id(0)
    t = pl.program_id(1)
    nt = pl.num_programs(1)

    @pl.when((b == 0) & (t == 0))
    def _():
        row = lax.broadcasted_iota(jnp.int32, (tq, tq), 0)
        col = lax.broadcasted_iota(jnp.int32, (tq, tq), 1)
        tri_scr[...] = jnp.where(col <= row, 1.0, 0.0).astype(BF16)
        for j in range(CONV_WIDTH):
            wb_scr[j * SUBLANES:(j + 1) * SUBLANES, :] = jnp.broadcast_to(wdw_ref[j:j + 1, :], (SUBLANES, cdim))

    @pl.when(t == 0)
    def _():
        ubuf[0:HIST, :] = jnp.zeros((HIST, cdim), F32)
        carry_scr[...] = jnp.zeros_like(carry_scr)
        carryt_scr[...] = jnp.zeros_like(carryt_scr)
        kn2_scr[...] = jnp.zeros_like(kn2_scr)

    x = x_ref[0]
    h_scr[...] = _rmsnorm(x, g_ref[...]).astype(BF16)

    def proj(off, width):
        return _dot(h_scr[...], wmain_ref[:, off:off + width])

    o_q = 3 * cdim
    u = proj(0, cdim) * _sigmoid(proj(cdim, cdim))
    ubuf[HIST:HIST + tq, :] = u

    shift0 = HIST - (CONV_WIDTH - 1)
    groups = CONV_ROWS // SUBLANES
    sub = lax.broadcasted_iota(jnp.int32, (SUBLANES, LANES), 0)

    def conv_step(i, carry):
        r0 = pl.multiple_of(i * CONV_ROWS, CONV_ROWS)
        for cb in range(cdim // LANES):
            cols = slice(cb * LANES, (cb + 1) * LANES)
            win = [ubuf[pl.ds(r0 + SUBLANES * g, SUBLANES), cols] for g in range(groups + 4)]
            out = [jnp.broadcast_to(bdw_ref[:, cols], (SUBLANES, LANES)) for _ in range(groups)]
            for res in range(SUBLANES):
                taps = [j for j in range(CONV_WIDTH) if (shift0 + j) % SUBLANES == res]
                wts = [wb_scr[j * SUBLANES:(j + 1) * SUBLANES, cols] for j in taps]
                part = []
                for g in range(groups + (res > 0)):
                    acc = None
                    for j, w in zip(taps, wts):
                        term = w * win[g + (shift0 + j) // SUBLANES]
                        acc = term if acc is None else acc + term
                    part.append(acc)
                for g in range(groups):
                    if res == 0:
                        out[g] = out[g] + part[g]
                    else:
                        sel = jnp.where(sub >= res, part[g], part[g + 1])
                        out[g] = out[g] + pltpu.roll(sel, SUBLANES - res, axis=0)
            for g in range(groups):
                c_scr[pl.ds(r0 + SUBLANES * g, SUBLANES), cols] = out[g]
        return carry

    lax.fori_loop(0, tq // CONV_ROWS, conv_step, 0)

    @pl.when(t == nt - 1)
    def _():
        cst_ref[0] = ubuf[HIST + tq - (CONV_WIDTH - 1):HIST + tq, :]

    ubuf[0:HIST, :] = ubuf[tq:tq + HIST, :]

    cact = _silu(_layernorm(c_scr[...], lng_ref[...], lnb_ref[...])) * _silu(proj(2 * cdim, cdim))
    yc_ref[0] = _dot(cact.astype(BF16), wco_ref[...])

    qb_ref[0] = (proj(o_q, adim) * (ATTN_SCALE * LOG2E)).astype(BF16)
    ga_ref[0] = _silu(proj(o_q + 3 * adim, adim))
    kt = _dot_nt(wkvt_ref[0:adim, :], h_scr[...])
    kt32_ref[nprev, 0] = kt
    if nprev:
        kt32_ref[0:nprev] = kprev_ref[...]
        vt32_ref[0:nprev] = vprev_ref[...]
    kb_ref[0] = proj(o_q + adim, adim).astype(BF16)
    kf = kt.astype(BF16).astype(F32)
    ksq = kf * kf * KNORM_SLACK
    hrow = lax.broadcasted_iota(jnp.int32, (N_HEADS, LANES), 0)
    kn2 = kn2_scr[...]
    for hd in range(N_HEADS):
        col = jnp.sum(ksq[hd * HEAD_DIM:(hd + 1) * HEAD_DIM, :], axis=0, keepdims=True)
        top = jnp.max(col, axis=1, keepdims=True)
        kn2 = jnp.where(hrow == hd, jnp.maximum(kn2, top), kn2)
    kn2_scr[...] = kn2
    kn2_ref[0, 0] = kn2
    vt = _dot_nt(wkvt_ref[adim:2 * adim, :], h_scr[...])
    vt32_ref[nprev, 0] = vt
    kc = vtb_ref.shape[-1]
    for c in range(tq // kc):
        vtb_ref[0, c] = vt[:, c * kc:(c + 1) * kc].astype(BF16)

    tri = tri_scr[...]
    lft = _log_sigmoid(_dot_nt(wft_ref[...], h_scr[...]) + bft_ref[...])[:N_HEADS, :]
    lft_ref[0] = lft
    cst = carryt_scr[...]
    for part in _split3(lft):
        cst = cst + _dot_nt(part, tri)
    carryt_scr[...] = cst[:, tq - 1:tq]
    cumt_ref[0] = cst
    lane = lax.broadcasted_iota(jnp.int32, (tq, LANES), 1)
    lf = _log_sigmoid(_dot(h_scr[...], wf_ref[...]) + bf_ref[...])
    lf = jnp.where(lane < N_HEADS, lf, 0.0)
    cs = carry_scr[...]
    for part in _split3(lf):
        cs = cs + _dot(tri, part)
    carry_scr[...] = cs[tq - 1:tq, :]
    cum_ref[0] = cs[:, :N_HEADS]


def _proj_conv(x, layer, p, prev_kv, *, tq, kc):
    bsz, seq, d = x.shape
    cdim = p["w_dw"].shape[-1]
    adim = N_HEADS * HEAD_DIM
    nt = seq // tq
    nprev = 0 if prev_kv is None else prev_kv[0].shape[0]
    assert seq % tq == 0 and tq % CONV_ROWS == 0 and tq >= HIST
    tile = lambda w: pl.BlockSpec((1, tq, w), lambda b, t: (b, t, 0))
    tile_t = lambda r: pl.BlockSpec((1, r, tq), lambda b, t: (b, 0, t))
    assert tq % kc == 0
    chunk_t = lambda r: pl.BlockSpec((1, tq // kc, r, kc), lambda b, t: (b, t, 0, 0))
    stack_t = lambda n: pl.BlockSpec((n, 1, adim, tq), lambda b, t: (0, b, 0, t))
    out_shape = (
        jax.ShapeDtypeStruct((nprev + 1, bsz, adim, seq), F32),
        jax.ShapeDtypeStruct((nprev + 1, bsz, adim, seq), F32),
        jax.ShapeDtypeStruct((bsz, N_HEADS, seq), F32),
        jax.ShapeDtypeStruct((bsz, seq, adim), BF16),
        jax.ShapeDtypeStruct((bsz, seq, adim), BF16),
        jax.ShapeDtypeStruct((bsz, seq // kc, adim, kc), BF16),
        jax.ShapeDtypeStruct((bsz, seq, adim), F32),
        jax.ShapeDtypeStruct((bsz, seq, N_HEADS), F32),
        jax.ShapeDtypeStruct((bsz, N_HEADS, seq), F32),
        jax.ShapeDtypeStruct((bsz, seq, d), F32),
        jax.ShapeDtypeStruct((bsz, CONV_WIDTH - 1, cdim), F32),
        jax.ShapeDtypeStruct((bsz, nt, N_HEADS, LANES), F32),
    )
    out_specs = (
        stack_t(nprev + 1), stack_t(nprev + 1), tile_t(N_HEADS), tile(adim), tile(adim), chunk_t(adim), tile(adim),
        tile(N_HEADS), tile_t(N_HEADS), tile(d),
        pl.BlockSpec((1, CONV_WIDTH - 1, cdim), lambda b, t: (b, 0, 0)),
        pl.BlockSpec((1, 1, N_HEADS, LANES), lambda b, t: (b, t, 0, 0)),
    )
    names = ("norm_g", "w_main", "w_kvt", "w_f", "b_f", "w_ft", "b_ft", "w_dw", "b_dw", "ln_g", "ln_b",
             "w_conv_out")
    prev = () if prev_kv is None else tuple(prev_kv)
    in_specs = [tile(d)] + [stack_t(nprev)] * len(prev) + [_const_spec(p[n].shape, layer) for n in names]
    scratch = [
        pltpu.VMEM((tq, d), BF16),
        pltpu.VMEM((HIST + tq, cdim), F32),
        pltpu.VMEM((tq, cdim), F32),
        pltpu.VMEM((CONV_WIDTH * SUBLANES, cdim), F32),
        pltpu.VMEM((tq, tq), BF16),
        pltpu.VMEM((1, LANES), F32),
        pltpu.VMEM((N_HEADS, 1), F32),
        pltpu.VMEM((N_HEADS, LANES), F32),
    ]
    return pl.pallas_call(
        functools.partial(_proj_conv_kernel, tq=tq, cdim=cdim, adim=adim, nprev=nprev),
        grid=(bsz, nt), in_specs=in_specs, out_specs=out_specs, out_shape=out_shape, scratch_shapes=scratch,
        compiler_params=pltpu.CompilerParams(dimension_semantics=("arbitrary", "arbitrary"),
                                             vmem_limit_bytes=VMEM_LIMIT),
        name="proj_conv",
    )(x, *prev, *(p[n] for n in names))


def _proj_conv_step_kernel(x_ref, st_ref, g_ref, wmain_ref, wf_ref, bf_ref, wdw_ref, bdw_ref, lng_ref, lnb_ref,
                           wco_ref, k32_ref, v32_ref, logf_ref, qb_ref, ga_ref, yc_ref, u_ref, *, cdim, adim):
    h = _rmsnorm(x_ref[...], g_ref[...]).astype(BF16)

    def proj(off, width):
        return _dot(h, wmain_ref[:, off:off + width])

    o_q = 3 * cdim
    u = proj(0, cdim) * _sigmoid(proj(cdim, cdim))
    u_ref[...] = u
    c = wdw_ref[CONV_WIDTH - 1:CONV_WIDTH, :] * u + bdw_ref[...]
    for j in range(CONV_WIDTH - 1):
        c = c + wdw_ref[j:j + 1, :] * st_ref[j]
    cact = _silu(_layernorm(c, lng_ref[...], lnb_ref[...])) * _silu(proj(2 * cdim, cdim))
    yc_ref[...] = _dot(cact.astype(BF16), wco_ref[...])
    qb_ref[...] = (proj(o_q, adim) * ATTN_SCALE).astype(BF16)
    k32_ref[...] = proj(o_q + adim, adim)
    v32_ref[...] = proj(o_q + 2 * adim, adim)
    ga_ref[...] = _silu(proj(o_q + 3 * adim, adim))
    lf = _log_sigmoid(_dot(h, wf_ref[...]) + bf_ref[...])
    logf_ref[...] = lf[:, :N_HEADS]


def _proj_conv_step(x, state_t, layer, p):
    rows, d = x.shape
    cdim = p["w_dw"].shape[-1]
    adim = N_HEADS * HEAD_DIM
    full = lambda shape: pl.BlockSpec(shape, lambda i: (0,) * len(shape))
    out_shape = (
        jax.ShapeDtypeStruct((rows, adim), F32), jax.ShapeDtypeStruct((rows, adim), F32),
        jax.ShapeDtypeStruct((rows, N_HEADS), F32), jax.ShapeDtypeStruct((rows, adim), BF16),
        jax.ShapeDtypeStruct((rows, adim), F32), jax.ShapeDtypeStruct((rows, d), F32),
        jax.ShapeDtypeStruct((rows, cdim), F32),
    )
    names = ("norm_g", "w_main", "w_f", "b_f", "w_dw", "b_dw", "ln_g", "ln_b", "w_conv_out")
    in_specs = [full(x.shape), _const_spec(state_t.shape, layer)] + [_const_spec(p[n].shape, layer) for n in names]
    return pl.pallas_call(
        functools.partial(_proj_conv_step_kernel, cdim=cdim, adim=adim),
        grid=(1,), in_specs=in_specs, out_specs=tuple(full(s.shape) for s in out_shape), out_shape=out_shape,
        compiler_params=pltpu.CompilerParams(dimension_semantics=("arbitrary",), vmem_limit_bytes=VMEM_LIMIT),
        name="proj_conv_step",
    )(x, state_t, *(p[n] for n in names))


def _fox_prompt_kernel(q_ref, k_ref, vt_ref, cum_ref, cumt_ref, cumend_ref, kn2_ref, ga_ref, og_ref, ckb_scr,
                       s_scr, smax_scr, *, tq, tk):
    hp = pl.program_id(1)
    qi = pl.program_id(2)
    q0 = qi * tq
    n_full = lax.div(q0, tk)
    nk = cumend_ref.shape[-1]
    kn2_rows = (nk * tk) // kn2_ref.shape[1]

    @pl.when(qi == 0)
    def _():
        hlane = lax.broadcasted_iota(jnp.int32, (1, N_HEADS), 1)

        def fill(c, carry):
            r0 = pl.multiple_of(c * tk, tk)
            blk = cum_ref[0, pl.ds(r0, tk), :]
            for e in range(HEADS_PER_BLOCK):
                col = jnp.sum(jnp.where(hlane == HEADS_PER_BLOCK * hp + e, blk, 0.0), axis=1, keepdims=True)
                ckb_scr[e, pl.ds(r0, tk), :] = jnp.broadcast_to(col * LOG2E, (tk, LANES))
            return carry

        lax.fori_loop(0, nk, fill, 0)

    q2 = q_ref[0]
    lane = lax.broadcasted_iota(jnp.int32, (1, LANES), 1)
    lower = lane < HEAD_DIM
    zero = jnp.zeros_like(q2)
    qs = (jnp.where(lower, q2, zero), jnp.where(lower, zero, q2))
    cqs = tuple(cumt_ref[0, pl.ds(HEADS_PER_BLOCK * hp + e, 1), :] * LOG2E
                for e in range(HEADS_PER_BLOCK))

    def scores(ki, causal=False):
        k0 = pl.multiple_of(ki * tk, tk)
        k2 = k_ref[0, pl.ds(k0, tk), :]
        out = []
        for e in range(HEADS_PER_BLOCK):
            ck = ckb_scr[e, pl.ds(k0, tk), :]
            s = _dot_nt(k2, qs[e]) - jnp.concatenate([ck] * (tq // LANES), axis=1)
            if causal:
                ahead = (lax.broadcasted_iota(jnp.int32, (tk, tq), 0)
                         - lax.broadcasted_iota(jnp.int32, (tk, tq), 1))
                s = jnp.where(ahead <= q0 - k0, s, NEG)
            out.append((s, jnp.max(s, axis=0, keepdims=True)))
        return out

    def update(ki, sc, state, valid=None):
        vt = vt_ref[0, ki]
        new = []
        for e in range(HEADS_PER_BLOCK):
            m_old, l_old, acc_old = state[3 * e:3 * e + 3]
            s, smax = sc[e]
            m_new = jnp.maximum(m_old, smax + cqs[e])
            if valid is not None:
                m_new = jnp.where(valid, m_new, m_old)
            p = jnp.exp2(s - (m_new - cqs[e]))
            alpha = jnp.exp2(m_old - m_new)
            l_new = alpha * l_old + jnp.sum(p, axis=0, keepdims=True)
            acc_new = alpha * acc_old + _dot(vt[e * HEAD_DIM:(e + 1) * HEAD_DIM, :], p.astype(BF16))
            if valid is not None:
                l_new = jnp.where(valid, l_new, l_old)
                acc_new = jnp.where(valid, acc_new, acc_old)
            new += [m_new, l_new, acc_new]
        return tuple(new)

    init = (jnp.full((1, tq), NEG, F32), jnp.zeros((1, tq), F32), jnp.zeros((HEAD_DIM, tq), F32)) * HEADS_PER_BLOCK

    def produce(buf, ki):
        for e, (s, smax) in enumerate(scores(jnp.maximum(ki, 0))):
            s_scr[buf, e] = s
            smax_scr[buf, e] = smax

    def consume(buf, ki, st):
        sc = [(s_scr[buf, e], smax_scr[buf, e]) for e in range(HEADS_PER_BLOCK)]
        return update(jnp.maximum(ki, 0), sc, st, valid=ki >= 0)

    diag = scores(n_full, causal=True)
    produce(0, n_full - 1)
    produce(1, n_full - 2)
    state = update(n_full, diag, init)

    qf = q2.astype(F32)
    qsq = qf * qf
    kidx = lax.broadcasted_iota(jnp.int32, (1, nk), 1)
    first_live = []
    for e in range(HEADS_PER_BLOCK):
        head = HEADS_PER_BLOCK * hp + e
        own_lanes = lower if e == 0 else jnp.logical_not(lower)
        qn2 = jnp.max(jnp.sum(jnp.where(own_lanes, qsq, 0.0), axis=1, keepdims=True), axis=0, keepdims=True)
        kn2 = kn2_ref[0, lax.div(q0, kn2_rows), pl.ds(head, 1), :][:, 0:1]
        slack = jnp.max(cqs[e] - state[3 * e], axis=1, keepdims=True)
        reach = jnp.sqrt(qn2 * kn2) + slack + PRUNE_T * LOG2E
        live = (cumend_ref[0, pl.ds(head, 1), :] * LOG2E <= reach) | (kidx >= n_full)
        first_live.append(jnp.min(jnp.where(live, kidx, nk).astype(F32)))
    ki_start = jnp.minimum(first_live[0], first_live[1]).astype(jnp.int32)

    npairs = lax.div(n_full - ki_start + 1, 2)

    def sweep(j, st):
        top = n_full - 1 - 2 * j
        st = consume(0, top, st)
        produce(0, top - 2)
        st = consume(1, top - 1, st)
        produce(1, top - 3)
        return st

    state = lax.fori_loop(0, npairs - 1, sweep, state)

    def last_pair(st):
        top = n_full + 1 - 2 * npairs
        return consume(1, top - 1, consume(0, top, st))

    state = lax.cond(npairs >= 1, last_pair, lambda st: st, state)

    o_t = jnp.concatenate([state[2] / state[1], state[5] / state[4]], axis=0)
    og_ref[0] = (o_t.T * ga_ref[0]).astype(BF16)


def _fox_prompt(qb, kb, vtb, cum, cumt, kn2, ga, *, tq):
    bsz, seq, adim = qb.shape
    nk, tk = vtb.shape[1], vtb.shape[3]
    assert seq % tq == 0 and tk % tq == 0 and nk * tk == seq
    nhp = adim // LANES
    cumend = cumt[:, :, tk - 1::tk]
    qtile = pl.BlockSpec((1, tq, LANES), lambda b, hp, qi: (b, qi, hp))
    return pl.pallas_call(
        functools.partial(_fox_prompt_kernel, tq=tq, tk=tk),
        grid=(bsz, nhp, seq // tq),
        in_specs=[qtile,
                  pl.BlockSpec((1, seq, LANES), lambda b, hp, qi: (b, 0, hp)),
                  pl.BlockSpec((1, nk, LANES, tk), lambda b, hp, qi: (b, 0, hp, 0)),
                  pl.BlockSpec((1, seq, N_HEADS), lambda b, hp, qi: (b, 0, 0)),
                  pl.BlockSpec((1, N_HEADS, tq), lambda b, hp, qi: (b, 0, qi)),
                  pl.BlockSpec((1, N_HEADS, nk), lambda b, hp, qi: (b, 0, 0)),
                  pl.BlockSpec((1, kn2.shape[1], N_HEADS, LANES), lambda b, hp, qi: (b, 0, 0, 0)),
                  qtile],
        out_specs=qtile,
        out_shape=jax.ShapeDtypeStruct((bsz, seq, adim), BF16),
        scratch_shapes=[pltpu.VMEM((HEADS_PER_BLOCK, seq, LANES), F32),
                        pltpu.VMEM((2, HEADS_PER_BLOCK, tk, tq), F32),
                        pltpu.VMEM((2, HEADS_PER_BLOCK, 1, tq), F32)],
        compiler_params=pltpu.CompilerParams(dimension_semantics=("arbitrary",) * 3, vmem_limit_bytes=VMEM_LIMIT),
        name="fox_prompt",
    )(qb, kb, vtb, cum, cumt, cumend, kn2, ga)


def _own_lanes(adim):
    sub = lax.broadcasted_iota(jnp.int32, (N_HEADS, adim), 0)
    lane_head = lax.broadcasted_iota(jnp.int32, (N_HEADS, adim), 1) // HEAD_DIM
    return sub == lane_head


def _fox_decode_scores_kernel(pt_ref, q_ref, kn_ref, lfn_ref, *rest, pages):
    k_refs = rest[:pages]
    lf_refs = rest[pages:2 * pages]
    pn_ref, pself_ref, live_ref, s_scr, m_scr, self_scr, carry_scr = rest[2 * pages:]
    del pt_ref
    j = pl.program_id(1)
    steps = pl.num_programs(1)
    adim = N_HEADS * HEAD_DIM

    qbd32 = jnp.where(_own_lanes(adim), jnp.broadcast_to(q_ref[0].astype(F32), (N_HEADS, adim)), 0.0)
    qbd = qbd32.astype(BF16)

    @pl.when(j == 0)
    def _():
        kn = kn_ref[0].astype(BF16).astype(F32)
        s_self = jnp.sum(qbd32 * kn, axis=1, keepdims=True)
        self_scr[...] = s_self
        m_scr[...] = s_self
        hs = lax.broadcasted_iota(jnp.int32, (N_HEADS, N_HEADS), 0)
        hl = lax.broadcasted_iota(jnp.int32, (N_HEADS, N_HEADS), 1)
        carry_scr[...] = jnp.sum(jnp.where(hs == hl, jnp.broadcast_to(lfn_ref[0], (N_HEADS, N_HEADS)), 0.0),
                                 axis=1, keepdims=True)

    lf_all = jnp.concatenate([r[...] for r in lf_refs], axis=0)
    jr = lax.broadcasted_iota(jnp.int32, (PAGE_SIZE, PAGE_SIZE), 0)
    sc = lax.broadcasted_iota(jnp.int32, (PAGE_SIZE, PAGE_SIZE), 1)
    later = jnp.where(jr > sc, 1.0, 0.0).astype(BF16)
    within = None
    for part in _split3(lf_all):
        w = _dot(part, later)
        within = w if within is None else within + w

    run = carry_scr[...]
    s_parts = [None] * pages
    for i in reversed(range(pages)):
        bias = within[i * N_HEADS:(i + 1) * N_HEADS, :] + run
        s_parts[i] = _dot(qbd, k_refs[i][...].astype(BF16)) + bias
        run = run + jnp.sum(lf_refs[i][...], axis=1, keepdims=True)
    carry_scr[...] = run

    s = jnp.concatenate(s_parts, axis=1)
    s_scr[steps - 1 - j] = s
    m_new = jnp.maximum(m_scr[...], jnp.max(s, axis=1, keepdims=True))
    m_scr[...] = m_new

    lane = lax.broadcasted_iota(jnp.int32, (1, LANES), 1)
    flags = jnp.zeros((1, LANES), F32)
    for i in range(pages):
        top = jnp.max(s_parts[i] - m_new, axis=1, keepdims=True)
        alive = jnp.max(jnp.exp(top), axis=0, keepdims=True)
        flags = jnp.where(lane == i, alive, flags)
    live_ref[0, 0] = flags

    @pl.when(j == steps - 1)
    def _():
        m = m_scr[...]
        p_self = jnp.exp(self_scr[...] - m)
        l = p_self
        for jj in range(s_scr.shape[0]):
            p = jnp.exp(s_scr[jj] - m)
            s_scr[jj] = p
            l = l + jnp.sum(p, axis=1, keepdims=True)
        inv = 1.0 / l
        for jj in range(s_scr.shape[0]):
            pn_ref[0, jj] = s_scr[jj] * inv
        pself_ref[0] = jnp.broadcast_to(p_self * inv, (N_HEADS, LANES))


def _fox_decode_pv_kernel(pt_ref, any_ref, pn_ref, pself_ref, vn_ref, ga_ref, *rest, pages):
    v_refs = rest[:pages]
    og_ref, acc_scr = rest[pages:]
    del pt_ref
    b = pl.program_id(0)
    j = pl.program_id(1)
    adim = N_HEADS * HEAD_DIM

    @pl.when(j == 0)
    def _():
        acc_scr[...] = jnp.zeros_like(acc_scr)

    @pl.when(any_ref[b, j] > 0)
    def _():
        pv = jnp.zeros((N_HEADS, adim), F32)
        per_group = pn_ref.shape[-1] // PAGE_SIZE
        for i in range(pages):
            g, c = divmod(i, per_group)
            pn = pn_ref[0, g, :, c * PAGE_SIZE:(c + 1) * PAGE_SIZE]
            pv = pv + _dot_nt(pn.astype(BF16), v_refs[i][...].astype(BF16))
        acc_scr[...] = acc_scr[...] + pv

    @pl.when(j == pl.num_programs(1) - 1)
    def _():
        acc = acc_scr[...] + pself_ref[0][:, 0:1] * vn_ref[0].astype(BF16).astype(F32)
        o = jnp.sum(jnp.where(_own_lanes(adim), acc, 0.0), axis=0, keepdims=True)
        og_ref[0] = (o * ga_ref[0]).astype(BF16)


def _fox_decode(qb, k_new, v_new, lf_new, ga, cache_kt, cache_vt, cache_lft, page_table, layer, *, pages):
    db, adim = qb.shape
    n_pages = page_table.shape[1]
    assert n_pages % pages == 0 and pages <= LANES
    steps = n_pages // pages
    span = pages * PAGE_SIZE
    r3 = lambda a: a.reshape(db, 1, a.shape[-1])
    params = pltpu.CompilerParams(dimension_semantics=("arbitrary", "arbitrary"), vmem_limit_bytes=VMEM_LIMIT)

    row = lambda w: pl.BlockSpec((1, 1, w), lambda b, j, pt: (b, 0, 0))

    def page_spec(i, rows):
        return pl.BlockSpec((None, None, rows, PAGE_SIZE),
                            lambda b, j, pt: (layer, pt[b, (steps - 1 - j) * pages + i], 0, 0))

    pn, pself, live = pl.pallas_call(
        functools.partial(_fox_decode_scores_kernel, pages=pages),
        grid_spec=pltpu.PrefetchScalarGridSpec(
            num_scalar_prefetch=1, grid=(db, steps),
            in_specs=([row(adim), row(adim), row(N_HEADS)]
                      + [page_spec(i, adim) for i in range(pages)]
                      + [page_spec(i, N_HEADS) for i in range(pages)]),
            out_specs=(pl.BlockSpec((1, steps, N_HEADS, span), lambda b, j, pt: (b, 0, 0, 0)),
                       pl.BlockSpec((1, N_HEADS, LANES), lambda b, j, pt: (b, 0, 0)),
                       pl.BlockSpec((1, 1, 1, LANES), lambda b, j, pt: (b, steps - 1 - j, 0, 0))),
            scratch_shapes=[pltpu.VMEM((steps, N_HEADS, span), F32), pltpu.VMEM((N_HEADS, 1), F32),
                            pltpu.VMEM((N_HEADS, 1), F32), pltpu.VMEM((N_HEADS, 1), F32)]),
        out_shape=(jax.ShapeDtypeStruct((db, steps, N_HEADS, span), F32),
                   jax.ShapeDtypeStruct((db, N_HEADS, LANES), F32),
                   jax.ShapeDtypeStruct((db, steps, 1, LANES), F32)),
        compiler_params=params, name="fox_decode_scores",
    )(page_table, r3(qb), r3(k_new), r3(lf_new), *([cache_kt] * pages), *([cache_lft] * pages))

    group = DECODE_VALUE_GROUP if steps % DECODE_VALUE_GROUP == 0 else 1
    vsteps, vpages = steps // group, pages * group
    alive = (live[:, :, 0, :pages] > 0.0).reshape(db * vsteps, vpages)
    order = jnp.arange(db * vsteps, dtype=jnp.int32)[:, None]
    last = jnp.maximum(lax.cummax(jnp.where(alive, order, -1), axis=0), 0)
    page_ids = jnp.take_along_axis(page_table.reshape(db * vsteps, vpages), last, axis=0).reshape(db, n_pages)
    any_alive = jnp.any(alive, axis=1).reshape(db, vsteps).astype(jnp.int32)

    row2 = lambda w: pl.BlockSpec((1, 1, w), lambda b, j, pt, al: (b, 0, 0))

    def value_spec(i):
        return pl.BlockSpec((None, None, adim, PAGE_SIZE), lambda b, j, pt, al: (layer, pt[b, j * vpages + i], 0, 0))

    og = pl.pallas_call(
        functools.partial(_fox_decode_pv_kernel, pages=vpages),
        grid_spec=pltpu.PrefetchScalarGridSpec(
            num_scalar_prefetch=2, grid=(db, vsteps),
            in_specs=([pl.BlockSpec((1, group, N_HEADS, span), lambda b, j, pt, al: (b, j, 0, 0)),
                       pl.BlockSpec((1, N_HEADS, LANES), lambda b, j, pt, al: (b, 0, 0)),
                       row2(adim), row2(adim)] + [value_spec(i) for i in range(vpages)]),
            out_specs=row2(adim),
            scratch_shapes=[pltpu.VMEM((N_HEADS, adim), F32)]),
        out_shape=jax.ShapeDtypeStruct((db, 1, adim), BF16),
        compiler_params=params, name="fox_decode_pv",
    )(page_ids, any_alive, pn, pself, r3(v_new), r3(ga), *([cache_vt] * vpages))
    return og.reshape(db, adim)


def _merge_out_kernel(x_ref, og_ref, yc_ref, g_ref, wmix_ref, wao_ref, wo_ref, fg_ref, out_ref, *, final):
    x = x_ref[0]
    d = x.shape[-1]
    h = _rmsnorm(x, g_ref[...]).astype(BF16)
    mix_c = _sigmoid(_dot(h, wmix_ref[:, 0:d]))
    mix_a = _sigmoid(_dot(h, wmix_ref[:, d:2 * d]))
    y_a = _dot(og_ref[0], wao_ref[...])
    m = mix_c * yc_ref[0] + mix_a * y_a
    y = x + _dot(m.astype(BF16), wo_ref[...])
    if final:
        y = _rmsnorm(y, fg_ref[...])
    out_ref[0] = y


def _merge_out(x, og, yc, layer, p, final_g, *, tq, final):
    bsz, seq, d = x.shape
    adim = og.shape[-1]
    assert seq % tq == 0
    tile = lambda w: pl.BlockSpec((1, tq, w), lambda b, t: (b, t, 0))
    in_specs = [tile(d), tile(adim), tile(d)] + [_const_spec(p[n].shape, layer) for n in
                                                  ("norm_g", "w_mix", "w_attn_out", "w_o")]
    in_specs.append(pl.BlockSpec(final_g.shape, lambda b, t: (0, 0)))
    return pl.pallas_call(
        functools.partial(_merge_out_kernel, final=final),
        grid=(bsz, seq // tq), in_specs=in_specs, out_specs=tile(d),
        out_shape=jax.ShapeDtypeStruct((bsz, seq, d), F32),
        compiler_params=pltpu.CompilerParams(dimension_semantics=("arbitrary", "arbitrary"),
                                             vmem_limit_bytes=VMEM_LIMIT),
        name="merge_out",
    )(x, og, yc, p["norm_g"], p["w_mix"], p["w_attn_out"], p["w_o"], final_g)


PROMPT_TILE = 512
ATTN_Q_TILE = 512
ATTN_K_CHUNK = 512
DECODE_PAGES = 16
DECODE_VALUE_GROUP = 2


def kernel(x_prompt, x_sample, cache_k, cache_v, cache_logf, state_conv, page_table, norm_g, w_in, b_f, w_dw,
           b_dw, ln_g, ln_b, w_conv_out, w_attn_out, w_o, final_g):
    depth, d = norm_g.shape
    bsz, seq, _ = x_prompt.shape
    db, dec_seq, _ = x_sample.shape
    assert dec_seq == 1
    cdim = w_dw.shape[-1]
    adim = N_HEADS * HEAD_DIM
    o_k = 3 * cdim + adim
    n_main = 3 * cdim + 4 * adim

    w_in_b = w_in.astype(BF16)
    w_f = w_in_b[:, :, n_main:n_main + N_HEADS]
    params = {
        "norm_g": norm_g.reshape(depth, 1, d),
        "w_main": w_in_b[:, :, :n_main],
        "w_kvt": jnp.swapaxes(w_in_b[:, :, o_k:o_k + 2 * adim], 1, 2),
        "w_f": jnp.pad(w_f, ((0, 0), (0, 0), (0, LANES - N_HEADS))),
        "w_ft": jnp.pad(jnp.swapaxes(w_f, 1, 2), ((0, 0), (0, BF16_ROWS - N_HEADS), (0, 0))),
        "w_mix": w_in_b[:, :, n_main + N_HEADS:],
        "b_f": jnp.pad(b_f, ((0, 0), (0, LANES - N_HEADS))).reshape(depth, 1, LANES),
        "b_ft": jnp.pad(b_f, ((0, 0), (0, BF16_ROWS - N_HEADS))).reshape(depth, BF16_ROWS, 1),
        "w_dw": w_dw,
        "b_dw": b_dw.reshape(depth, 1, cdim),
        "ln_g": ln_g.reshape(depth, 1, cdim),
        "ln_b": ln_b.reshape(depth, 1, cdim),
        "w_conv_out": w_conv_out.astype(BF16),
        "w_attn_out": w_attn_out.astype(BF16),
        "w_o": w_o.astype(BF16),
    }
    fg = final_g.reshape(1, d)
    n_pool = cache_k.shape[1]
    ckt = jnp.transpose(cache_k, (0, 1, 3, 4, 2)).reshape(depth, n_pool, adim, PAGE_SIZE)
    cvt = jnp.transpose(cache_v, (0, 1, 3, 4, 2)).reshape(depth, n_pool, adim, PAGE_SIZE)
    clft = jnp.transpose(cache_logf, (0, 1, 3, 2))
    state_t = jnp.transpose(state_conv, (0, 2, 1, 3))

    def heads_last(a_t):
        return jnp.transpose(a_t.reshape(depth, bsz, N_HEADS, HEAD_DIM, seq), (0, 1, 4, 2, 3))

    xp = x_prompt
    xs = x_sample.reshape(db, d)
    lp, cp, ksl, vsl, lsl, csl = ([] for _ in range(6))
    kv_stack = None
    for layer in range(depth):
        final = layer == depth - 1
        kt32, vt32, lft, qb, kb, vtb, ga, cum, cumt, yc, cst, kn2 = _proj_conv(xp, layer, params, kv_stack,
                                                                                tq=PROMPT_TILE, kc=ATTN_K_CHUNK)
        kv_stack = (kt32, vt32)
        og = _fox_prompt(qb, kb, vtb, cum, cumt, kn2, ga, tq=ATTN_Q_TILE)
        xp = _merge_out(xp, og, yc, layer, params, fg, tq=PROMPT_TILE, final=final)
        lp.append(jnp.transpose(lft, (0, 2, 1)))
        cp.append(cst)

        k_s, v_s, lf_s, q_s, ga_s, yc_s, u_s = _proj_conv_step(xs, state_t, layer, params)
        og_s = _fox_decode(q_s, k_s, v_s, lf_s, ga_s, ckt, cvt, clft, page_table, layer, pages=DECODE_PAGES)
        xs = _merge_out(xs[None], og_s[None], yc_s[None], layer, params, fg, tq=db, final=final)[0]
        ksl.append(k_s.reshape(db, 1, N_HEADS, HEAD_DIM))
        vsl.append(v_s.reshape(db, 1, N_HEADS, HEAD_DIM))
        lsl.append(lf_s.reshape(db, 1, N_HEADS))
        csl.append(jnp.transpose(jnp.concatenate([state_t[layer, 1:], u_s[None]], axis=0), (1, 0, 2)))

    return (xp, xs.reshape(db, 1, d), heads_last(kv_stack[0]), heads_last(kv_stack[1]), jnp.stack(lp), jnp.stack(cp),
            jnp.stack(ksl), jnp.stack(vsl), jnp.stack(lsl), jnp.stack(csl))
```

```python
import functools

import jax
import jax.numpy as jnp
from jax import lax
from jax.experimental import pallas as pl
from jax.experimental.pallas import tpu as pltpu

CONV_WIDTH = 31
N_HEADS = 8
HEAD_DIM = 64
PAGE_SIZE = 128
RMS_EPS = 1e-6
LN_EPS = 1e-5
ATTN_SCALE = HEAD_DIM ** -0.5
LOG2E = 1.4426950408889634

LANES = 128
SUBLANES = 8
BF16_ROWS = 16
HEADS_PER_BLOCK = LANES // HEAD_DIM
VMEM_LIMIT = 56 * 1024 * 1024

HIST = 32
CONV_ROWS = 64
NEG = -1e30
PRUNE_T = 90.0
KNORM_SLACK = (1.0 + 2.0 ** -7) ** 2
DEAD_LOGIT = -104.0

BF16 = jnp.bfloat16
F32 = jnp.float32


def _sigmoid(x):
    return 0.5 * jnp.tanh(0.5 * x) + 0.5


def _silu(x):
    return x * _sigmoid(x)


def _log_sigmoid(x):
    return jnp.minimum(x, 0.0) - jnp.log1p(jnp.exp(-jnp.abs(x)))


def _rmsnorm(x, g):
    return x * lax.rsqrt(jnp.mean(x * x, axis=-1, keepdims=True) + RMS_EPS) * g


def _layernorm(x, g, b):
    mu = jnp.mean(x, axis=-1, keepdims=True)
    xc = x - mu
    var = jnp.mean(xc * xc, axis=-1, keepdims=True)
    return xc * lax.rsqrt(var + LN_EPS) * g + b


def _dot(a, b):
    return jnp.dot(a, b, preferred_element_type=F32)


def _dot_nt(a, b):
    return lax.dot_general(a, b, (((1,), (1,)), ((), ())), preferred_element_type=F32)


def _split3(x):
    hi = x.astype(BF16)
    r = x - hi.astype(F32)
    mid = r.astype(BF16)
    lo = (r - mid.astype(F32)).astype(BF16)
    return hi, mid, lo


def _const_spec(shape, layer):
    nd = len(shape) - 1
    return pl.BlockSpec((None,) + tuple(shape[1:]), lambda *_: (layer,) + (0,) * nd)


def _proj_conv_kernel(*refs, tq, cdim, adim, nprev):
    x_ref = refs[0]
    kprev_ref, vprev_ref = refs[1:3] if nprev else (None, None)
    (g_ref, wmain_ref, wkvt_ref, wf_ref, bf_ref, wft_ref, bft_ref, wdw_ref, bdw_ref, lng_ref, lnb_ref, wco_ref,
     kt32_ref, vt32_ref, lft_ref, qb_ref, kb_ref, vtb_ref, ga_ref, cum_ref, cumt_ref, yc_ref, cst_ref, kn2_ref,
     h_scr, ubuf, c_scr, wb_scr, tri_scr, carry_scr, carryt_scr, kn2_scr) = refs[1 + 2 * bool(nprev):]
    b = pl.program_id(0)
    t = pl.program_id(1)
    nt = pl.num_programs(1)

    @pl.when((b == 0) & (t == 0))
    def _():
        row = lax.broadcasted_iota(jnp.int32, (tq, tq), 0)
        col = lax.broadcasted_iota(jnp.int32, (tq, tq), 1)
        tri_scr[...] = jnp.where(col <= row, 1.0, 0.0).astype(BF16)
        for j in range(CONV_WIDTH):
            wb_scr[j * SUBLANES:(j + 1) * SUBLANES, :] = jnp.broadcast_to(wdw_ref[j:j + 1, :], (SUBLANES, cdim))

    @pl.when(t == 0)
    def _():
        ubuf[0:HIST, :] = jnp.zeros((HIST, cdim), F32)
        carry_scr[...] = jnp.zeros_like(carry_scr)
        carryt_scr[...] = jnp.zeros_like(carryt_scr)
        kn2_scr[...] = jnp.zeros_like(kn2_scr)

    x = x_ref[0]
    h_scr[...] = _rmsnorm(x, g_ref[...]).astype(BF16)

    def proj(off, width):
        return _dot(h_scr[...], wmain_ref[:, off:off + width])

    o_q = 3 * cdim
    u = proj(0, cdim) * _sigmoid(proj(cdim, cdim))
    ubuf[HIST:HIST + tq, :] = u

    shift0 = HIST - (CONV_WIDTH - 1)
    groups = CONV_ROWS // SUBLANES
    sub = lax.broadcasted_iota(jnp.int32, (SUBLANES, LANES), 0)

    def conv_step(i, carry):
        r0 = pl.multiple_of(i * CONV_ROWS, CONV_ROWS)
        for cb in range(cdim // LANES):
            cols = slice(cb * LANES, (cb + 1) * LANES)
            win = [ubuf[pl.ds(r0 + SUBLANES * g, SUBLANES), cols] for g in range(groups + 4)]
            out = [jnp.broadcast_to(bdw_ref[:, cols], (SUBLANES, LANES)) for _ in range(groups)]
            for res in range(SUBLANES):
                taps = [j for j in range(CONV_WIDTH) if (shift0 + j) % SUBLANES == res]
                wts = [wb_scr[j * SUBLANES:(j + 1) * SUBLANES, cols] for j in taps]
                part = []
                for g in range(groups + (res > 0)):
                    acc = None
                    for j, w in zip(taps, wts):
                        term = w * win[g + (shift0 + j) // SUBLANES]
                        acc = term if acc is None else acc + term
                    part.append(acc)
                for g in range(groups):
                    if res == 0:
                        out[g] = out[g] + part[g]
                    else:
                        sel = jnp.where(sub >= res, part[g], part[g + 1])
                        out[g] = out[g] + pltpu.roll(sel, SUBLANES - res, axis=0)
            for g in range(groups):
                c_scr[pl.ds(r0 + SUBLANES * g, SUBLANES), cols] = out[g]
        return carry

    lax.fori_loop(0, tq // CONV_ROWS, conv_step, 0)

    @pl.when(t == nt - 1)
    def _():
        cst_ref[0] = ubuf[HIST + tq - (CONV_WIDTH - 1):HIST + tq, :]

    ubuf[0:HIST, :] = ubuf[tq:tq + HIST, :]

    cact = _silu(_layernorm(c_scr[...], lng_ref[...], lnb_ref[...])) * _silu(proj(2 * cdim, cdim))
    yc_ref[0] = _dot(cact.astype(BF16), wco_ref[...])

    qb_ref[0] = (proj(o_q, adim) * (ATTN_SCALE * LOG2E)).astype(BF16)
    ga_ref[0] = _silu(proj(o_q + 3 * adim, adim))
    kt = _dot_nt(wkvt_ref[0:adim, :], h_scr[...])
    kt32_ref[nprev, 0] = kt
    if nprev:
        kt32_ref[0:nprev] = kprev_ref[...]
        vt32_ref[0:nprev] = vprev_ref[...]
    kb_ref[0] = kt.T.astype(BF16)
    kf = kt.astype(BF16).astype(F32)
    ksq = kf * kf * KNORM_SLACK
    hrow = lax.broadcasted_iota(jnp.int32, (N_HEADS, LANES), 0)
    kn2 = kn2_scr[...]
    for hd in range(N_HEADS):
        col = jnp.sum(ksq[hd * HEAD_DIM:(hd + 1) * HEAD_DIM, :], axis=0, keepdims=True)
        top = jnp.max(col, axis=1, keepdims=True)
        kn2 = jnp.where(hrow == hd, jnp.maximum(kn2, top), kn2)
    kn2_scr[...] = kn2
    kn2_ref[0, 0] = kn2
    vt = _dot_nt(wkvt_ref[adim:2 * adim, :], h_scr[...])
    vt32_ref[nprev, 0] = vt
    kc = vtb_ref.shape[-1]
    for c in range(tq // kc):
        vtb_ref[0, c] = vt[:, c * kc:(c + 1) * kc].astype(BF16)

    tri = tri_scr[...]
    lft = _log_sigmoid(_dot_nt(wft_ref[...], h_scr[...]) + bft_ref[...])[:N_HEADS, :]
    lft_ref[0] = lft
    cst = carryt_scr[...]
    for part in _split3(lft):
        cst = cst + _dot_nt(part, tri)
    carryt_scr[...] = cst[:, tq - 1:tq]
    cumt_ref[0] = cst
    lane = lax.broadcasted_iota(jnp.int32, (tq, LANES), 1)
    lf = _log_sigmoid(_dot(h_scr[...], wf_ref[...]) + bf_ref[...])
    lf = jnp.where(lane < N_HEADS, lf, 0.0)
    cs = carry_scr[...]
    for part in _split3(lf):
        cs = cs + _dot(tri, part)
    carry_scr[...] = cs[tq - 1:tq, :]
    cum_ref[0] = cs[:, :N_HEADS]


def _proj_conv(x, layer, p, prev_kv, *, tq, kc):
    bsz, seq, d = x.shape
    cdim = p["w_dw"].shape[-1]
    adim = N_HEADS * HEAD_DIM
    nt = seq // tq
    nprev = 0 if prev_kv is None else prev_kv[0].shape[0]
    assert seq % tq == 0 and tq % CONV_ROWS == 0 and tq >= HIST
    tile = lambda w: pl.BlockSpec((1, tq, w), lambda b, t: (b, t, 0))
    tile_t = lambda r: pl.BlockSpec((1, r, tq), lambda b, t: (b, 0, t))
    assert tq % kc == 0
    chunk_t = lambda r: pl.BlockSpec((1, tq // kc, r, kc), lambda b, t: (b, t, 0, 0))
    stack_t = lambda n: pl.BlockSpec((n, 1, adim, tq), lambda b, t: (0, b, 0, t))
    out_shape = (
        jax.ShapeDtypeStruct((nprev + 1, bsz, adim, seq), F32),
        jax.ShapeDtypeStruct((nprev + 1, bsz, adim, seq), F32),
        jax.ShapeDtypeStruct((bsz, N_HEADS, seq), F32),
        jax.ShapeDtypeStruct((bsz, seq, adim), BF16),
        jax.ShapeDtypeStruct((bsz, seq, adim), BF16),
        jax.ShapeDtypeStruct((bsz, seq // kc, adim, kc), BF16),
        jax.ShapeDtypeStruct((bsz, seq, adim), F32),
        jax.ShapeDtypeStruct((bsz, seq, N_HEADS), F32),
        jax.ShapeDtypeStruct((bsz, N_HEADS, seq), F32),
        jax.ShapeDtypeStruct((bsz, seq, d), F32),
        jax.ShapeDtypeStruct((bsz, CONV_WIDTH - 1, cdim), F32),
        jax.ShapeDtypeStruct((bsz, nt, N_HEADS, LANES), F32),
    )
    out_specs = (
        stack_t(nprev + 1), stack_t(nprev + 1), tile_t(N_HEADS), tile(adim), tile(adim), chunk_t(adim), tile(adim),
        tile(N_HEADS), tile_t(N_HEADS), tile(d),
        pl.BlockSpec((1, CONV_WIDTH - 1, cdim), lambda b, t: (b, 0, 0)),
        pl.BlockSpec((1, 1, N_HEADS, LANES), lambda b, t: (b, t, 0, 0)),
    )
    names = ("norm_g", "w_main", "w_kvt", "w_f", "b_f", "w_ft", "b_ft", "w_dw", "b_dw", "ln_g", "ln_b",
             "w_conv_out")
    prev = () if prev_kv is None else tuple(prev_kv)
    in_specs = [tile(d)] + [stack_t(nprev)] * len(prev) + [_const_spec(p[n].shape, layer) for n in names]
    scratch = [
        pltpu.VMEM((tq, d), BF16),
        pltpu.VMEM((HIST + tq, cdim), F32),
        pltpu.VMEM((tq, cdim), F32),
        pltpu.VMEM((CONV_WIDTH * SUBLANES, cdim), F32),
        pltpu.VMEM((tq, tq), BF16),
        pltpu.VMEM((1, LANES), F32),
        pltpu.VMEM((N_HEADS, 1), F32),
        pltpu.VMEM((N_HEADS, LANES), F32),
    ]
    return pl.pallas_call(
        functools.partial(_proj_conv_kernel, tq=tq, cdim=cdim, adim=adim, nprev=nprev),
        grid=(bsz, nt), in_specs=in_specs, out_specs=out_specs, out_shape=out_shape, scratch_shapes=scratch,
        compiler_params=pltpu.CompilerParams(dimension_semantics=("arbitrary", "arbitrary"),
                                             vmem_limit_bytes=VMEM_LIMIT),
        name="proj_conv",
    )(x, *prev, *(p[n] for n in names))


def _proj_conv_step_kernel(x_ref, st_ref, g_ref, wmain_ref, wf_ref, bf_ref, wdw_ref, bdw_ref, lng_ref, lnb_ref,
                           wco_ref, k32_ref, v32_ref, logf_ref, qb_ref, ga_ref, yc_ref, u_ref, *, cdim, adim):
    h = _rmsnorm(x_ref[...], g_ref[...]).astype(BF16)

    def proj(off, width):
        return _dot(h, wmain_ref[:, off:off + width])

    o_q = 3 * cdim
    u = proj(0, cdim) * _sigmoid(proj(cdim, cdim))
    u_ref[...] = u
    c = wdw_ref[CONV_WIDTH - 1:CONV_WIDTH, :] * u + bdw_ref[...]
    for j in range(CONV_WIDTH - 1):
        c = c + wdw_ref[j:j + 1, :] * st_ref[j]
    cact = _silu(_layernorm(c, lng_ref[...], lnb_ref[...])) * _silu(proj(2 * cdim, cdim))
    yc_ref[...] = _dot(cact.astype(BF16), wco_ref[...])
    qb_ref[...] = (proj(o_q, adim) * ATTN_SCALE).astype(BF16)
    k32_ref[...] = proj(o_q + adim, adim)
    v32_ref[...] = proj(o_q + 2 * adim, adim)
    ga_ref[...] = _silu(proj(o_q + 3 * adim, adim))
    lf = _log_sigmoid(_dot(h, wf_ref[...]) + bf_ref[...])
    logf_ref[...] = lf[:, :N_HEADS]


def _proj_conv_step(x, state_t, layer, p):
    rows, d = x.shape
    cdim = p["w_dw"].shape[-1]
    adim = N_HEADS * HEAD_DIM
    full = lambda shape: pl.BlockSpec(shape, lambda i: (0,) * len(shape))
    out_shape = (
        jax.ShapeDtypeStruct((rows, adim), F32), jax.ShapeDtypeStruct((rows, adim), F32),
        jax.ShapeDtypeStruct((rows, N_HEADS), F32), jax.ShapeDtypeStruct((rows, adim), BF16),
        jax.ShapeDtypeStruct((rows, adim), F32), jax.ShapeDtypeStruct((rows, d), F32),
        jax.ShapeDtypeStruct((rows, cdim), F32),
    )
    names = ("norm_g", "w_main", "w_f", "b_f", "w_dw", "b_dw", "ln_g", "ln_b", "w_conv_out")
    in_specs = [full(x.shape), _const_spec(state_t.shape, layer)] + [_const_spec(p[n].shape, layer) for n in names]
    return pl.pallas_call(
        functools.partial(_proj_conv_step_kernel, cdim=cdim, adim=adim),
        grid=(1,), in_specs=in_specs, out_specs=tuple(full(s.shape) for s in out_shape), out_shape=out_shape,
        compiler_params=pltpu.CompilerParams(dimension_semantics=("arbitrary",), vmem_limit_bytes=VMEM_LIMIT),
        name="proj_conv_step",
    )(x, state_t, *(p[n] for n in names))


def _fox_prompt_kernel(q_ref, k_ref, vt_ref, cum_ref, cumt_ref, cumend_ref, kn2_ref, ga_ref, og_ref, ckb_scr,
                       s_scr, smax_scr, *, tq, tk):
    hp = pl.program_id(1)
    qi = pl.program_id(2)
    q0 = qi * tq
    n_full = lax.div(q0, tk)
    nk = cumend_ref.shape[-1]
    kn2_rows = (nk * tk) // kn2_ref.shape[1]

    @pl.when(qi == 0)
    def _():
        hlane = lax.broadcasted_iota(jnp.int32, (1, N_HEADS), 1)

        def fill(c, carry):
            r0 = pl.multiple_of(c * tk, tk)
            blk = cum_ref[0, pl.ds(r0, tk), :]
            for e in range(HEADS_PER_BLOCK):
                col = jnp.sum(jnp.where(hlane == HEADS_PER_BLOCK * hp + e, blk, 0.0), axis=1, keepdims=True)
                ckb_scr[e, pl.ds(r0, tk), :] = jnp.broadcast_to(col * LOG2E, (tk, LANES))
            return carry

        lax.fori_loop(0, nk, fill, 0)

    q2 = q_ref[0]
    lane = lax.broadcasted_iota(jnp.int32, (1, LANES), 1)
    lower = lane < HEAD_DIM
    zero = jnp.zeros_like(q2)
    qs = (jnp.where(lower, q2, zero), jnp.where(lower, zero, q2))
    cqs = tuple(cumt_ref[0, pl.ds(HEADS_PER_BLOCK * hp + e, 1), :] * LOG2E
                for e in range(HEADS_PER_BLOCK))

    def scores(ki, causal=False):
        k0 = pl.multiple_of(ki * tk, tk)
        k2 = k_ref[0, pl.ds(k0, tk), :]
        out = []
        for e in range(HEADS_PER_BLOCK):
            ck = ckb_scr[e, pl.ds(k0, tk), :]
            s = _dot_nt(k2, qs[e]) - jnp.concatenate([ck] * (tq // LANES), axis=1)
            if causal:
                ahead = (lax.broadcasted_iota(jnp.int32, (tk, tq), 0)
                         - lax.broadcasted_iota(jnp.int32, (tk, tq), 1))
                s = jnp.where(ahead <= q0 - k0, s, NEG)
            out.append((s, jnp.max(s, axis=0, keepdims=True)))
        return out

    def update(ki, sc, state, valid=None):
        vt = vt_ref[0, ki]
        new = []
        for e in range(HEADS_PER_BLOCK):
            m_old, l_old, acc_old = state[3 * e:3 * e + 3]
            s, smax = sc[e]
            m_new = jnp.maximum(m_old, smax + cqs[e])
            if valid is not None:
                m_new = jnp.where(valid, m_new, m_old)
            p = jnp.exp2(s - (m_new - cqs[e]))
            alpha = jnp.exp2(m_old - m_new)
            l_new = alpha * l_old + jnp.sum(p, axis=0, keepdims=True)
            acc_new = alpha * acc_old + _dot(vt[e * HEAD_DIM:(e + 1) * HEAD_DIM, :], p.astype(BF16))
            if valid is not None:
                l_new = jnp.where(valid, l_new, l_old)
                acc_new = jnp.where(valid, acc_new, acc_old)
            new += [m_new, l_new, acc_new]
        return tuple(new)

    init = (jnp.full((1, tq), NEG, F32), jnp.zeros((1, tq), F32), jnp.zeros((HEAD_DIM, tq), F32)) * HEADS_PER_BLOCK

    def produce(buf, ki):
        for e, (s, smax) in enumerate(scores(jnp.maximum(ki, 0))):
            s_scr[buf, e] = s
            smax_scr[buf, e] = smax

    def consume(buf, ki, st):
        sc = [(s_scr[buf, e], smax_scr[buf, e]) for e in range(HEADS_PER_BLOCK)]
        return update(jnp.maximum(ki, 0), sc, st, valid=ki >= 0)

    diag = scores(n_full, causal=True)
    produce(0, n_full - 1)
    produce(1, n_full - 2)
    state = update(n_full, diag, init)

    qf = q2.astype(F32)
    qsq = qf * qf
    kidx = lax.broadcasted_iota(jnp.int32, (1, nk), 1)
    first_live = []
    for e in range(HEADS_PER_BLOCK):
        head = HEADS_PER_BLOCK * hp + e
        own_lanes = lower if e == 0 else jnp.logical_not(lower)
        qn2 = jnp.max(jnp.sum(jnp.where(own_lanes, qsq, 0.0), axis=1, keepdims=True), axis=0, keepdims=True)
        kn2 = kn2_ref[0, lax.div(q0, kn2_rows), pl.ds(head, 1), :][:, 0:1]
        slack = jnp.max(cqs[e] - state[3 * e], axis=1, keepdims=True)
        reach = jnp.sqrt(qn2 * kn2) + slack + PRUNE_T * LOG2E
        live = (cumend_ref[0, pl.ds(head, 1), :] * LOG2E <= reach) | (kidx >= n_full)
        first_live.append(jnp.min(jnp.where(live, kidx, nk).astype(F32)))
    ki_start = jnp.minimum(first_live[0], first_live[1]).astype(jnp.int32)

    npairs = lax.div(n_full - ki_start + 1, 2)

    def sweep(j, st):
        top = n_full - 1 - 2 * j
        st = consume(0, top, st)
        produce(0, top - 2)
        st = consume(1, top - 1, st)
        produce(1, top - 3)
        return st

    state = lax.fori_loop(0, npairs - 1, sweep, state)

    def last_pair(st):
        top = n_full + 1 - 2 * npairs
        st = consume(0, top, st)
        return lax.cond(top - 1 >= ki_start, lambda s: consume(1, top - 1, s), lambda s: s, st)

    state = lax.cond(npairs >= 1, last_pair, lambda st: st, state)

    o_t = jnp.concatenate([state[2] / state[1], state[5] / state[4]], axis=0)
    og_ref[0] = (o_t.T * ga_ref[0]).astype(BF16)


def _fox_prompt(qb, kb, vtb, cum, cumt, kn2, ga, *, tq):
    bsz, seq, adim = qb.shape
    nk, tk = vtb.shape[1], vtb.shape[3]
    assert seq % tq == 0 and tk % tq == 0 and nk * tk == seq
    nhp = adim // LANES
    cumend = cumt[:, :, tk - 1::tk]
    qtile = pl.BlockSpec((1, tq, LANES), lambda b, hp, qi: (b, qi, hp))
    return pl.pallas_call(
        functools.partial(_fox_prompt_kernel, tq=tq, tk=tk),
        grid=(bsz, nhp, seq // tq),
        in_specs=[qtile,
                  pl.BlockSpec((1, seq, LANES), lambda b, hp, qi: (b, 0, hp)),
                  pl.BlockSpec((1, nk, LANES, tk), lambda b, hp, qi: (b, 0, hp, 0)),
                  pl.BlockSpec((1, seq, N_HEADS), lambda b, hp, qi: (b, 0, 0)),
                  pl.BlockSpec((1, N_HEADS, tq), lambda b, hp, qi: (b, 0, qi)),
                  pl.BlockSpec((1, N_HEADS, nk), lambda b, hp, qi: (b, 0, 0)),
                  pl.BlockSpec((1, kn2.shape[1], N_HEADS, LANES), lambda b, hp, qi: (b, 0, 0, 0)),
                  qtile],
        out_specs=qtile,
        out_shape=jax.ShapeDtypeStruct((bsz, seq, adim), BF16),
        scratch_shapes=[pltpu.VMEM((HEADS_PER_BLOCK, seq, LANES), F32),
                        pltpu.VMEM((2, HEADS_PER_BLOCK, tk, tq), F32),
                        pltpu.VMEM((2, HEADS_PER_BLOCK, 1, tq), F32)],
        compiler_params=pltpu.CompilerParams(dimension_semantics=("arbitrary",) * 3, vmem_limit_bytes=VMEM_LIMIT),
        name="fox_prompt",
    )(qb, kb, vtb, cum, cumt, cumend, kn2, ga)


def _own_lanes(adim):
    sub = lax.broadcasted_iota(jnp.int32, (N_HEADS, adim), 0)
    lane_head = lax.broadcasted_iota(jnp.int32, (N_HEADS, adim), 1) // HEAD_DIM
    return sub == lane_head


def _fox_decode_scores_kernel(pt_ref, q_ref, kn_ref, lfn_ref, *rest, pages):
    k_refs = rest[:pages]
    lf_refs = rest[pages:2 * pages]
    pn_ref, pself_ref, live_ref, s_scr, m_scr, self_scr, carry_scr = rest[2 * pages:]
    del pt_ref
    j = pl.program_id(1)
    steps = pl.num_programs(1)
    adim = N_HEADS * HEAD_DIM

    qbd32 = jnp.where(_own_lanes(adim), jnp.broadcast_to(q_ref[0].astype(F32), (N_HEADS, adim)), 0.0)
    qbd = qbd32.astype(BF16)

    @pl.when(j == 0)
    def _():
        kn = kn_ref[0].astype(BF16).astype(F32)
        s_self = jnp.sum(qbd32 * kn, axis=1, keepdims=True)
        self_scr[...] = s_self
        m_scr[...] = s_self
        hs = lax.broadcasted_iota(jnp.int32, (N_HEADS, N_HEADS), 0)
        hl = lax.broadcasted_iota(jnp.int32, (N_HEADS, N_HEADS), 1)
        carry_scr[...] = jnp.sum(jnp.where(hs == hl, jnp.broadcast_to(lfn_ref[0], (N_HEADS, N_HEADS)), 0.0),
                                 axis=1, keepdims=True)

    lf_all = jnp.concatenate([r[...] for r in lf_refs], axis=0)
    jr = lax.broadcasted_iota(jnp.int32, (PAGE_SIZE, PAGE_SIZE), 0)
    sc = lax.broadcasted_iota(jnp.int32, (PAGE_SIZE, PAGE_SIZE), 1)
    later = jnp.where(jr > sc, 1.0, 0.0).astype(BF16)
    within = None
    for part in _split3(lf_all):
        w = _dot(part, later)
        within = w if within is None else within + w

    run = carry_scr[...]
    s_parts = [None] * pages
    for i in reversed(range(pages)):
        bias = within[i * N_HEADS:(i + 1) * N_HEADS, :] + run
        s_parts[i] = _dot(qbd, k_refs[i][...].astype(BF16)) + bias
        run = run + jnp.sum(lf_refs[i][...], axis=1, keepdims=True)
    carry_scr[...] = run

    s = jnp.concatenate(s_parts, axis=1)
    s_scr[steps - 1 - j] = s
    m_new = jnp.maximum(m_scr[...], jnp.max(s, axis=1, keepdims=True))
    m_scr[...] = m_new

    live_ref[0, 0] = jnp.max(s - m_new, axis=0, keepdims=True)

    @pl.when(j == steps - 1)
    def _():
        m = m_scr[...]
        p_self = jnp.exp(self_scr[...] - m)
        l = p_self
        for jj in range(s_scr.shape[0]):
            p = jnp.exp(s_scr[jj] - m)
            s_scr[jj] = p
            l = l + jnp.sum(p, axis=1, keepdims=True)
        inv = 1.0 / l
        for jj in range(s_scr.shape[0]):
            pn_ref[0, jj] = s_scr[jj] * inv
        pself_ref[0] = jnp.broadcast_to(p_self * inv, (N_HEADS, LANES))


def _fox_decode_pv_kernel(pt_ref, any_ref, pn_ref, pself_ref, vn_ref, ga_ref, *rest, pages):
    v_refs = rest[:pages]
    og_ref, acc_scr = rest[pages:]
    del pt_ref
    b = pl.program_id(0)
    j = pl.program_id(1)
    adim = N_HEADS * HEAD_DIM

    @pl.when(j == 0)
    def _():
        acc_scr[...] = jnp.zeros_like(acc_scr)

    @pl.when(any_ref[b, j] > 0)
    def _():
        pv = jnp.zeros((N_HEADS, adim), F32)
        per_group = pn_ref.shape[-1] // PAGE_SIZE
        for i in range(pages):
            g, c = divmod(i, per_group)
            pn = pn_ref[0, g, :, c * PAGE_SIZE:(c + 1) * PAGE_SIZE]
            pv = pv + _dot_nt(pn.astype(BF16), v_refs[i][...].astype(BF16))
        acc_scr[...] = acc_scr[...] + pv

    @pl.when(j == pl.num_programs(1) - 1)
    def _():
        acc = acc_scr[...] + pself_ref[0][:, 0:1] * vn_ref[0].astype(BF16).astype(F32)
        o = jnp.sum(jnp.where(_own_lanes(adim), acc, 0.0), axis=0, keepdims=True)
        og_ref[0] = (o * ga_ref[0]).astype(BF16)


def _fox_decode(qb, k_new, v_new, lf_new, ga, cache_kt, cache_vt, cache_lft, page_table, layer, *, pages):
    db, adim = qb.shape
    n_pages = page_table.shape[1]
    assert n_pages % pages == 0 and pages <= LANES
    steps = n_pages // pages
    span = pages * PAGE_SIZE
    r3 = lambda a: a.reshape(db, 1, a.shape[-1])
    params = pltpu.CompilerParams(dimension_semantics=("arbitrary", "arbitrary"), vmem_limit_bytes=VMEM_LIMIT)

    row = lambda w: pl.BlockSpec((1, 1, w), lambda b, j, pt: (b, 0, 0))

    def page_spec(i, rows):
        return pl.BlockSpec((None, None, rows, PAGE_SIZE),
                            lambda b, j, pt: (layer, pt[b, (steps - 1 - j) * pages + i], 0, 0))

    pn, pself, live = pl.pallas_call(
        functools.partial(_fox_decode_scores_kernel, pages=pages),
        grid_spec=pltpu.PrefetchScalarGridSpec(
            num_scalar_prefetch=1, grid=(db, steps),
            in_specs=([row(adim), row(adim), row(N_HEADS)]
                      + [page_spec(i, adim) for i in range(pages)]
                      + [page_spec(i, N_HEADS) for i in range(pages)]),
            out_specs=(pl.BlockSpec((1, steps, N_HEADS, span), lambda b, j, pt: (b, 0, 0, 0)),
                       pl.BlockSpec((1, N_HEADS, LANES), lambda b, j, pt: (b, 0, 0)),
                       pl.BlockSpec((1, 1, 1, span), lambda b, j, pt: (b, steps - 1 - j, 0, 0))),
            scratch_shapes=[pltpu.VMEM((steps, N_HEADS, span), F32), pltpu.VMEM((N_HEADS, 1), F32),
                            pltpu.VMEM((N_HEADS, 1), F32), pltpu.VMEM((N_HEADS, 1), F32)]),
        out_shape=(jax.ShapeDtypeStruct((db, steps, N_HEADS, span), F32),
                   jax.ShapeDtypeStruct((db, N_HEADS, LANES), F32),
                   jax.ShapeDtypeStruct((db, steps, 1, span), F32)),
        compiler_params=params, name="fox_decode_scores",
    )(page_table, r3(qb), r3(k_new), r3(lf_new), *([cache_kt] * pages), *([cache_lft] * pages))

    group = DECODE_VALUE_GROUP if steps % DECODE_VALUE_GROUP == 0 else 1
    vsteps, vpages = steps // group, pages * group
    alive = (jnp.max(live.reshape(db, n_pages, PAGE_SIZE), axis=-1) >= DEAD_LOGIT).reshape(db * vsteps, vpages)
    order = jnp.arange(db * vsteps, dtype=jnp.int32)[:, None]
    last = jnp.maximum(lax.cummax(jnp.where(alive, order, -1), axis=0), 0)
    page_ids = jnp.take_along_axis(page_table.reshape(db * vsteps, vpages), last, axis=0).reshape(db, n_pages)
    any_alive = jnp.any(alive, axis=1).reshape(db, vsteps).astype(jnp.int32)

    row2 = lambda w: pl.BlockSpec((1, 1, w), lambda b, j, pt, al: (b, 0, 0))

    def value_spec(i):
        return pl.BlockSpec((None, None, adim, PAGE_SIZE), lambda b, j, pt, al: (layer, pt[b, j * vpages + i], 0, 0))

    og = pl.pallas_call(
        functools.partial(_fox_decode_pv_kernel, pages=vpages),
        grid_spec=pltpu.PrefetchScalarGridSpec(
            num_scalar_prefetch=2, grid=(db, vsteps),
            in_specs=([pl.BlockSpec((1, group, N_HEADS, span), lambda b, j, pt, al: (b, j, 0, 0)),
                       pl.BlockSpec((1, N_HEADS, LANES), lambda b, j, pt, al: (b, 0, 0)),
                       row2(adim), row2(adim)] + [value_spec(i) for i in range(vpages)]),
            out_specs=row2(adim),
            scratch_shapes=[pltpu.VMEM((N_HEADS, adim), F32)]),
        out_shape=jax.ShapeDtypeStruct((db, 1, adim), BF16),
        compiler_params=params, name="fox_decode_pv",
    )(page_ids, any_alive, pn, pself, r3(v_new), r3(ga), *([cache_vt] * vpages))
    return og.reshape(db, adim)


def _merge_out_kernel(x_ref, og_ref, yc_ref, g_ref, wmix_ref, wao_ref, wo_ref, fg_ref, out_ref, *, final):
    x = x_ref[0]
    d = x.shape[-1]
    h = _rmsnorm(x, g_ref[...]).astype(BF16)
    mix_c = _sigmoid(_dot(h, wmix_ref[:, 0:d]))
    mix_a = _sigmoid(_dot(h, wmix_ref[:, d:2 * d]))
    y_a = _dot(og_ref[0], wao_ref[...])
    m = mix_c * yc_ref[0] + mix_a * y_a
    y = x + _dot(m.astype(BF16), wo_ref[...])
    if final:
        y = _rmsnorm(y, fg_ref[...])
    out_ref[0] = y


def _merge_out(x, og, yc, layer, p, final_g, *, tq, final):
    bsz, seq, d = x.shape
    adim = og.shape[-1]
    assert seq % tq == 0
    tile = lambda w: pl.BlockSpec((1, tq, w), lambda b, t: (b, t, 0))
    in_specs = [tile(d), tile(adim), tile(d)] + [_const_spec(p[n].shape, layer) for n in
                                                  ("norm_g", "w_mix", "w_attn_out", "w_o")]
    in_specs.append(pl.BlockSpec(final_g.shape, lambda b, t: (0, 0)))
    return pl.pallas_call(
        functools.partial(_merge_out_kernel, final=final),
        grid=(bsz, seq // tq), in_specs=in_specs, out_specs=tile(d),
        out_shape=jax.ShapeDtypeStruct((bsz, seq, d), F32),
        compiler_params=pltpu.CompilerParams(dimension_semantics=("arbitrary", "arbitrary"),
                                             vmem_limit_bytes=VMEM_LIMIT),
        name="merge_out",
    )(x, og, yc, p["norm_g"], p["w_mix"], p["w_attn_out"], p["w_o"], final_g)


PROMPT_TILE = 512
ATTN_Q_TILE = 512
ATTN_K_CHUNK = 512
DECODE_PAGES = 16
DECODE_VALUE_GROUP = 2


def kernel(x_prompt, x_sample, cache_k, cache_v, cache_logf, state_conv, page_table, norm_g, w_in, b_f, w_dw,
           b_dw, ln_g, ln_b, w_conv_out, w_attn_out, w_o, final_g):
    depth, d = norm_g.shape
    bsz, seq, _ = x_prompt.shape
    db, dec_seq, _ = x_sample.shape
    assert dec_seq == 1
    cdim = w_dw.shape[-1]
    adim = N_HEADS * HEAD_DIM
    o_k = 3 * cdim + adim
    n_main = 3 * cdim + 4 * adim

    w_in_b = w_in.astype(BF16)
    w_f = w_in_b[:, :, n_main:n_main + N_HEADS]
    params = {
        "norm_g": norm_g.reshape(depth, 1, d),
        "w_main": w_in_b[:, :, :n_main],
        "w_kvt": jnp.swapaxes(w_in_b[:, :, o_k:o_k + 2 * adim], 1, 2),
        "w_f": jnp.pad(w_f, ((0, 0), (0, 0), (0, LANES - N_HEADS))),
        "w_ft": jnp.pad(jnp.swapaxes(w_f, 1, 2), ((0, 0), (0, BF16_ROWS - N_HEADS), (0, 0))),
        "w_mix": w_in_b[:, :, n_main + N_HEADS:],
        "b_f": jnp.pad(b_f, ((0, 0), (0, LANES - N_HEADS))).reshape(depth, 1, LANES),
        "b_ft": jnp.pad(b_f, ((0, 0), (0, BF16_ROWS - N_HEADS))).reshape(depth, BF16_ROWS, 1),
        "w_dw": w_dw,
        "b_dw": b_dw.reshape(depth, 1, cdim),
        "ln_g": ln_g.reshape(depth, 1, cdim),
        "ln_b": ln_b.reshape(depth, 1, cdim),
        "w_conv_out": w_conv_out.astype(BF16),
        "w_attn_out": w_attn_out.astype(BF16),
        "w_o": w_o.astype(BF16),
    }
    fg = final_g.reshape(1, d)
    n_pool = cache_k.shape[1]
    ckt = jnp.transpose(cache_k, (0, 1, 3, 4, 2)).reshape(depth, n_pool, adim, PAGE_SIZE)
    cvt = jnp.transpose(cache_v, (0, 1, 3, 4, 2)).reshape(depth, n_pool, adim, PAGE_SIZE)
    clft = jnp.transpose(cache_logf, (0, 1, 3, 2))
    state_t = jnp.transpose(state_conv, (0, 2, 1, 3))

    def heads_last(a_t):
        return jnp.transpose(a_t.reshape(depth, bsz, N_HEADS, HEAD_DIM, seq), (0, 1, 4, 2, 3))

    xp = x_prompt
    xs = x_sample.reshape(db, d)
    lp, cp, ksl, vsl, lsl, csl = ([] for _ in range(6))
    kv_stack = None
    for layer in range(depth):
        final = layer == depth - 1
        kt32, vt32, lft, qb, kb, vtb, ga, cum, cumt, yc, cst, kn2 = _proj_conv(xp, layer, params, kv_stack,
                                                                                tq=PROMPT_TILE, kc=ATTN_K_CHUNK)
        kv_stack = (kt32, vt32)
        og = _fox_prompt(qb, kb, vtb, cum, cumt, kn2, ga, tq=ATTN_Q_TILE)
        xp = _merge_out(xp, og, yc, layer, params, fg, tq=PROMPT_TILE, final=final)
        lp.append(jnp.transpose(lft, (0, 2, 1)))
        cp.append(cst)

        k_s, v_s, lf_s, q_s, ga_s, yc_s, u_s = _proj_conv_step(xs, state_t, layer, params)
        og_s = _fox_decode(q_s, k_s, v_s, lf_s, ga_s, ckt, cvt, clft, page_table, layer, pages=DECODE_PAGES)
        xs = _merge_out(xs[None], og_s[None], yc_s[None], layer, params, fg, tq=db, final=final)[0]
        ksl.append(k_s.reshape(db, 1, N_HEADS, HEAD_DIM))
        vsl.append(v_s.reshape(db, 1, N_HEADS, HEAD_DIM))
        lsl.append(lf_s.reshape(db, 1, N_HEADS))
        csl.append(jnp.transpose(jnp.concatenate([state_t[layer, 1:], u_s[None]], axis=0), (1, 0, 2)))

    return (xp, xs.reshape(db, 1, d), heads_last(kv_stack[0]), heads_last(kv_stack[1]), jnp.stack(lp), jnp.stack(cp),
            jnp.stack(ksl), jnp.stack(vsl), jnp.stack(lsl), jnp.stack(csl))
```

```python
import functools

import jax
import jax.numpy as jnp
from jax import lax
from jax.experimental import pallas as pl
from jax.experimental.pallas import tpu as pltpu

CONV_WIDTH = 31
N_HEADS = 8
HEAD_DIM = 64
PAGE_SIZE = 128
RMS_EPS = 1e-6
LN_EPS = 1e-5
ATTN_SCALE = HEAD_DIM ** -0.5
LOG2E = 1.4426950408889634

LANES = 128
SUBLANES = 8
BF16_ROWS = 16
HEADS_PER_BLOCK = LANES // HEAD_DIM
VMEM_LIMIT = 56 * 1024 * 1024

HIST = 32
CONV_ROWS = 64
NEG = -1e30
PRUNE_T = 90.0
KNORM_SLACK = (1.0 + 2.0 ** -7) ** 2
DEAD_LOGIT = -104.0

BF16 = jnp.bfloat16
F32 = jnp.float32


def _sigmoid(x):
    return 0.5 * jnp.tanh(0.5 * x) + 0.5


def _silu(x):
    return x * _sigmoid(x)


def _log_sigmoid(x):
    return jnp.minimum(x, 0.0) - jnp.log1p(jnp.exp(-jnp.abs(x)))


def _rmsnorm(x, g):
    return x * lax.rsqrt(jnp.mean(x * x, axis=-1, keepdims=True) + RMS_EPS) * g


def _layernorm(x, g, b):
    mu = jnp.mean(x, axis=-1, keepdims=True)
    xc = x - mu
    var = jnp.mean(xc * xc, axis=-1, keepdims=True)
    return xc * lax.rsqrt(var + LN_EPS) * g + b


def _dot(a, b):
    return jnp.dot(a, b, preferred_element_type=F32)


def _dot_nt(a, b):
    return lax.dot_general(a, b, (((1,), (1,)), ((), ())), preferred_element_type=F32)


def _split3(x):
    hi = x.astype(BF16)
    r = x - hi.astype(F32)
    mid = r.astype(BF16)
    lo = (r - mid.astype(F32)).astype(BF16)
    return hi, mid, lo


def _const_spec(shape, layer):
    nd = len(shape) - 1
    return pl.BlockSpec((None,) + tuple(shape[1:]), lambda *_: (layer,) + (0,) * nd)


def _proj_conv_kernel(*refs, tq, cdim, adim, nprev):
    x_ref = refs[0]
    kprev_ref, vprev_ref = refs[1:3] if nprev else (None, None)
    (g_ref, wmain_ref, wkvt_ref, wf_ref, bf_ref, wft_ref, bft_ref, wdw_ref, bdw_ref, lng_ref, lnb_ref, wco_ref,
     kt32_ref, vt32_ref, lft_ref, qb_ref, kb_ref, vtb_ref, ga_ref, cum_ref, cumt_ref, yc_ref, cst_ref, kn2_ref,
     h_scr, ubuf, c_scr, wb_scr, tri_scr, carry_scr, carryt_scr, kn2_scr) = refs[1 + 2 * bool(nprev):]
    b = pl.program_id(0)
    t = pl.program_id(1)
    nt = pl.num_programs(1)

    @pl.when((b == 0) & (t == 0))
    def _():
        row = lax.broadcasted_iota(jnp.int32, (tq, tq), 0)
        col = lax.broadcasted_iota(jnp.int32, (tq, tq), 1)
        tri_scr[...] = jnp.where(col <= row, 1.0, 0.0).astype(BF16)
        for j in range(CONV_WIDTH):
            wb_scr[j * SUBLANES:(j + 1) * SUBLANES, :] = jnp.broadcast_to(wdw_ref[j:j + 1, :], (SUBLANES, cdim))

    @pl.when(t == 0)
    def _():
        ubuf[0:HIST, :] = jnp.zeros((HIST, cdim), F32)
        carry_scr[...] = jnp.zeros_like(carry_scr)
        carryt_scr[...] = jnp.zeros_like(carryt_scr)
        kn2_scr[...] = jnp.zeros_like(kn2_scr)

    x = x_ref[0]
    h_scr[...] = _rmsnorm(x, g_ref[...]).astype(BF16)

    def proj(off, width):
        return _dot(h_scr[...], wmain_ref[:, off:off + width])

    o_q = 3 * cdim
    u = proj(0, cdim) * _sigmoid(proj(cdim, cdim))
    ubuf[HIST:HIST + tq, :] = u

    shift0 = HIST - (CONV_WIDTH - 1)
    groups = CONV_ROWS // SUBLANES
    sub = lax.broadcasted_iota(jnp.int32, (SUBLANES, LANES), 0)

    def conv_step(i, carry):
        r0 = pl.multiple_of(i * CONV_ROWS, CONV_ROWS)
        for cb in range(cdim // LANES):
            cols = slice(cb * LANES, (cb + 1) * LANES)
            win = [ubuf[pl.ds(r0 + SUBLANES * g, SUBLANES), cols] for g in range(groups + 4)]
            out = [jnp.broadcast_to(bdw_ref[:, cols], (SUBLANES, LANES)) for _ in range(groups)]
            for res in range(SUBLANES):
                taps = [j for j in range(CONV_WIDTH) if (shift0 + j) % SUBLANES == res]
                wts = [wb_scr[j * SUBLANES:(j + 1) * SUBLANES, cols] for j in taps]
                part = []
                for g in range(groups + (res > 0)):
                    acc = None
                    for j, w in zip(taps, wts):
                        term = w * win[g + (shift0 + j) // SUBLANES]
                        acc = term if acc is None else acc + term
                    part.append(acc)
                for g in range(groups):
                    if res == 0:
                        out[g] = out[g] + part[g]
                    else:
                        sel = jnp.where(sub >= res, part[g], part[g + 1])
                        out[g] = out[g] + pltpu.roll(sel, SUBLANES - res, axis=0)
            for g in range(groups):
                c_scr[pl.ds(r0 + SUBLANES * g, SUBLANES), cols] = out[g]
        return carry

    lax.fori_loop(0, tq // CONV_ROWS, conv_step, 0)

    @pl.when(t == nt - 1)
    def _():
        cst_ref[0] = ubuf[HIST + tq - (CONV_WIDTH - 1):HIST + tq, :]

    ubuf[0:HIST, :] = ubuf[tq:tq + HIST, :]

    cact = _silu(_layernorm(c_scr[...], lng_ref[...], lnb_ref[...])) * _silu(proj(2 * cdim, cdim))
    yc_ref[0] = _dot(cact.astype(BF16), wco_ref[...])

    qb_ref[0] = (proj(o_q, adim) * (ATTN_SCALE * LOG2E)).astype(BF16)
    ga_ref[0] = _silu(proj(o_q + 3 * adim, adim))
    kt = _dot_nt(wkvt_ref[0:adim, :], h_scr[...])
    kt32_ref[nprev, 0] = kt
    if nprev:
        kt32_ref[0:nprev] = kprev_ref[...]
        vt32_ref[0:nprev] = vprev_ref[...]
    kb_ref[0] = kt.T.astype(BF16)
    kf = kt.astype(BF16).astype(F32)
    ksq = kf * kf * KNORM_SLACK
    hrow = lax.broadcasted_iota(jnp.int32, (N_HEADS, LANES), 0)
    kn2 = kn2_scr[...]
    for hd in range(N_HEADS):
        col = jnp.sum(ksq[hd * HEAD_DIM:(hd + 1) * HEAD_DIM, :], axis=0, keepdims=True)
        top = jnp.max(col, axis=1, keepdims=True)
        kn2 = jnp.where(hrow == hd, jnp.maximum(kn2, top), kn2)
    kn2_scr[...] = kn2
    kn2_ref[0, 0] = kn2
    vt = _dot_nt(wkvt_ref[adim:2 * adim, :], h_scr[...])
    vt32_ref[nprev, 0] = vt
    kc = vtb_ref.shape[-1]
    for c in range(tq // kc):
        vtb_ref[0, c] = vt[:, c * kc:(c + 1) * kc].astype(BF16)

    tri = tri_scr[...]
    lft = _log_sigmoid(_dot_nt(wft_ref[...], h_scr[...]) + bft_ref[...])[:N_HEADS, :]
    lft_ref[0] = lft
    cst = carryt_scr[...]
    for part in _split3(lft):
        cst = cst + _dot_nt(part, tri)
    carryt_scr[...] = cst[:, tq - 1:tq]
    cumt_ref[0] = cst
    lane = lax.broadcasted_iota(jnp.int32, (tq, LANES), 1)
    lf = _log_sigmoid(_dot(h_scr[...], wf_ref[...]) + bf_ref[...])
    lf = jnp.where(lane < N_HEADS, lf, 0.0)
    cs = carry_scr[...]
    for part in _split3(lf):
        cs = cs + _dot(tri, part)
    carry_scr[...] = cs[tq - 1:tq, :]
    cum_ref[0] = cs[:, :N_HEADS]


def _proj_conv(x, layer, p, prev_kv, *, tq, kc):
    bsz, seq, d = x.shape
    cdim = p["w_dw"].shape[-1]
    adim = N_HEADS * HEAD_DIM
    nt = seq // tq
    nprev = 0 if prev_kv is None else prev_kv[0].shape[0]
    assert seq % tq == 0 and tq % CONV_ROWS == 0 and tq >= HIST
    tile = lambda w: pl.BlockSpec((1, tq, w), lambda b, t: (b, t, 0))
    tile_t = lambda r: pl.BlockSpec((1, r, tq), lambda b, t: (b, 0, t))
    assert tq % kc == 0
    chunk_t = lambda r: pl.BlockSpec((1, tq // kc, r, kc), lambda b, t: (b, t, 0, 0))
    stack_t = lambda n: pl.BlockSpec((n, 1, adim, tq), lambda b, t: (0, b, 0, t))
    out_shape = (
        jax.ShapeDtypeStruct((nprev + 1, bsz, adim, seq), F32),
        jax.ShapeDtypeStruct((nprev + 1, bsz, adim, seq), F32),
        jax.ShapeDtypeStruct((bsz, N_HEADS, seq), F32),
        jax.ShapeDtypeStruct((bsz, seq, adim), BF16),
        jax.ShapeDtypeStruct((bsz, seq, adim), BF16),
        jax.ShapeDtypeStruct((bsz, seq // kc, adim, kc), BF16),
        jax.ShapeDtypeStruct((bsz, seq, adim), F32),
        jax.ShapeDtypeStruct((bsz, seq, N_HEADS), F32),
        jax.ShapeDtypeStruct((bsz, N_HEADS, seq), F32),
        jax.ShapeDtypeStruct((bsz, seq, d), F32),
        jax.ShapeDtypeStruct((bsz, CONV_WIDTH - 1, cdim), F32),
        jax.ShapeDtypeStruct((bsz, nt, N_HEADS, LANES), F32),
    )
    out_specs = (
        stack_t(nprev + 1), stack_t(nprev + 1), tile_t(N_HEADS), tile(adim), tile(adim), chunk_t(adim), tile(adim),
        tile(N_HEADS), tile_t(N_HEADS), tile(d),
        pl.BlockSpec((1, CONV_WIDTH - 1, cdim), lambda b, t: (b, 0, 0)),
        pl.BlockSpec((1, 1, N_HEADS, LANES), lambda b, t: (b, t, 0, 0)),
    )
    names = ("norm_g", "w_main", "w_kvt", "w_f", "b_f", "w_ft", "b_ft", "w_dw", "b_dw", "ln_g", "ln_b",
             "w_conv_out")
    prev = () if prev_kv is None else tuple(prev_kv)
    in_specs = [tile(d)] + [stack_t(nprev)] * len(prev) + [_const_spec(p[n].shape, layer) for n in names]
    scratch = [
        pltpu.VMEM((tq, d), BF16),
        pltpu.VMEM((HIST + tq, cdim), F32),
        pltpu.VMEM((tq, cdim), F32),
        pltpu.VMEM((CONV_WIDTH * SUBLANES, cdim), F32),
        pltpu.VMEM((tq, tq), BF16),
        pltpu.VMEM((1, LANES), F32),
        pltpu.VMEM((N_HEADS, 1), F32),
        pltpu.VMEM((N_HEADS, LANES), F32),
    ]
    return pl.pallas_call(
        functools.partial(_proj_conv_kernel, tq=tq, cdim=cdim, adim=adim, nprev=nprev),
        grid=(bsz, nt), in_specs=in_specs, out_specs=out_specs, out_shape=out_shape, scratch_shapes=scratch,
        compiler_params=pltpu.CompilerParams(dimension_semantics=("arbitrary", "arbitrary"),
                                             vmem_limit_bytes=VMEM_LIMIT),
        name="proj_conv",
    )(x, *prev, *(p[n] for n in names))


def _proj_conv_step_kernel(x_ref, st_ref, g_ref, wmain_ref, wf_ref, bf_ref, wdw_ref, bdw_ref, lng_ref, lnb_ref,
                           wco_ref, k32_ref, v32_ref, logf_ref, qb_ref, ga_ref, yc_ref, u_ref, *, cdim, adim):
    h = _rmsnorm(x_ref[...], g_ref[...]).astype(BF16)

    def proj(off, width):
        return _dot(h, wmain_ref[:, off:off + width])

    o_q = 3 * cdim
    u = proj(0, cdim) * _sigmoid(proj(cdim, cdim))
    u_ref[...] = u
    c = wdw_ref[CONV_WIDTH - 1:CONV_WIDTH, :] * u + bdw_ref[...]
    for j in range(CONV_WIDTH - 1):
        c = c + wdw_ref[j:j + 1, :] * st_ref[j]
    cact = _silu(_layernorm(c, lng_ref[...], lnb_ref[...])) * _silu(proj(2 * cdim, cdim))
    yc_ref[...] = _dot(cact.astype(BF16), wco_ref[...])
    qb_ref[...] = (proj(o_q, adim) * ATTN_SCALE).astype(BF16)
    k32_ref[...] = proj(o_q + adim, adim)
    v32_ref[...] = proj(o_q + 2 * adim, adim)
    ga_ref[...] = _silu(proj(o_q + 3 * adim, adim))
    lf = _log_sigmoid(_dot(h, wf_ref[...]) + bf_ref[...])
    logf_ref[...] = lf[:, :N_HEADS]


def _proj_conv_step(x, state_t, layer, p):
    rows, d = x.shape
    cdim = p["w_dw"].shape[-1]
    adim = N_HEADS * HEAD_DIM
    full = lambda shape: pl.BlockSpec(shape, lambda i: (0,) * len(shape))
    out_shape = (
        jax.ShapeDtypeStruct((rows, adim), F32), jax.ShapeDtypeStruct((rows, adim), F32),
        jax.ShapeDtypeStruct((rows, N_HEADS), F32), jax.ShapeDtypeStruct((rows, adim), BF16),
        jax.ShapeDtypeStruct((rows, adim), F32), jax.ShapeDtypeStruct((rows, d), F32),
        jax.ShapeDtypeStruct((rows, cdim), F32),
    )
    names = ("norm_g", "w_main", "w_f", "b_f", "w_dw", "b_dw", "ln_g", "ln_b", "w_conv_out")
    in_specs = [full(x.shape), _const_spec(state_t.shape, layer)] + [_const_spec(p[n].shape, layer) for n in names]
    return pl.pallas_call(
        functools.partial(_proj_conv_step_kernel, cdim=cdim, adim=adim),
        grid=(1,), in_specs=in_specs, out_specs=tuple(full(s.shape) for s in out_shape), out_shape=out_shape,
        compiler_params=pltpu.CompilerParams(dimension_semantics=("arbitrary",), vmem_limit_bytes=VMEM_LIMIT),
        name="proj_conv_step",
    )(x, state_t, *(p[n] for n in names))


def _fox_prompt_kernel(q_ref, k_ref, vt_ref, cum_ref, cumt_ref, cumend_ref, kn2_ref, ga_ref, og_ref, ckb_scr,
                       s_scr, smax_scr, *, tq, tk):
    hp = pl.program_id(1)
    qi = pl.program_id(2)
    q0 = qi * tq
    n_full = lax.div(q0, tk)
    nk = cumend_ref.shape[-1]
    kn2_rows = (nk * tk) // kn2_ref.shape[1]

    @pl.when(qi == 0)
    def _():
        hlane = lax.broadcasted_iota(jnp.int32, (1, N_HEADS), 1)

        def fill(c, carry):
            r0 = pl.multiple_of(c * tk, tk)
            blk = cum_ref[0, pl.ds(r0, tk), :]
            for e in range(HEADS_PER_BLOCK):
                col = jnp.sum(jnp.where(hlane == HEADS_PER_BLOCK * hp + e, blk, 0.0), axis=1, keepdims=True)
                ckb_scr[e, pl.ds(r0, tk), :] = jnp.broadcast_to(col * LOG2E, (tk, LANES))
            return carry

        lax.fori_loop(0, nk, fill, 0)

    q2 = q_ref[0]
    lane = lax.broadcasted_iota(jnp.int32, (1, LANES), 1)
    lower = lane < HEAD_DIM
    zero = jnp.zeros_like(q2)
    qs = (jnp.where(lower, q2, zero), jnp.where(lower, zero, q2))
    cqs = tuple(cumt_ref[0, pl.ds(HEADS_PER_BLOCK * hp + e, 1), :] * LOG2E
                for e in range(HEADS_PER_BLOCK))

    def scores(ki, causal=False):
        k0 = pl.multiple_of(ki * tk, tk)
        k2 = k_ref[0, pl.ds(k0, tk), :]
        out = []
        for e in range(HEADS_PER_BLOCK):
            ck = ckb_scr[e, pl.ds(k0, tk), :]
            s = _dot_nt(k2, qs[e]) - jnp.concatenate([ck] * (tq // LANES), axis=1)
            if causal:
                ahead = (lax.broadcasted_iota(jnp.int32, (tk, tq), 0)
                         - lax.broadcasted_iota(jnp.int32, (tk, tq), 1))
                s = jnp.where(ahead <= q0 - k0, s, NEG)
            out.append((s, jnp.max(s, axis=0, keepdims=True)))
        return out

    def update(ki, sc, state, valid=None):
        vt = vt_ref[0, ki]
        new = []
        for e in range(HEADS_PER_BLOCK):
            m_old, l_old, acc_old = state[3 * e:3 * e + 3]
            s, smax = sc[e]
            m_new = jnp.maximum(m_old, smax + cqs[e])
            if valid is not None:
                m_new = jnp.where(valid, m_new, m_old)
            p = jnp.exp2(s - (m_new - cqs[e]))
            alpha = jnp.exp2(m_old - m_new)
            l_new = alpha * l_old + jnp.sum(p, axis=0, keepdims=True)
            acc_new = alpha * acc_old + _dot(vt[e * HEAD_DIM:(e + 1) * HEAD_DIM, :], p.astype(BF16))
            if valid is not None:
                l_new = jnp.where(valid, l_new, l_old)
                acc_new = jnp.where(valid, acc_new, acc_old)
            new += [m_new, l_new, acc_new]
        return tuple(new)

    init = (jnp.full((1, tq), NEG, F32), jnp.zeros((1, tq), F32), jnp.zeros((HEAD_DIM, tq), F32)) * HEADS_PER_BLOCK

    def produce(buf, ki):
        for e, (s, smax) in enumerate(scores(jnp.maximum(ki, 0))):
            s_scr[buf, e] = s
            smax_scr[buf, e] = smax

    def consume(buf, ki, st):
        sc = [(s_scr[buf, e], smax_scr[buf, e]) for e in range(HEADS_PER_BLOCK)]
        return update(jnp.maximum(ki, 0), sc, st, valid=ki >= 0)

    diag = scores(n_full, causal=True)
    produce(0, n_full - 1)
    produce(1, n_full - 2)
    state = update(n_full, diag, init)

    qf = q2.astype(F32)
    qsq = qf * qf
    kidx = lax.broadcasted_iota(jnp.int32, (1, nk), 1)
    first_live = []
    for e in range(HEADS_PER_BLOCK):
        head = HEADS_PER_BLOCK * hp + e
        own_lanes = lower if e == 0 else jnp.logical_not(lower)
        qn2 = jnp.max(jnp.sum(jnp.where(own_lanes, qsq, 0.0), axis=1, keepdims=True), axis=0, keepdims=True)
        kn2 = kn2_ref[0, lax.div(q0, kn2_rows), pl.ds(head, 1), :][:, 0:1]
        slack = jnp.max(cqs[e] - state[3 * e], axis=1, keepdims=True)
        reach = jnp.sqrt(qn2 * kn2) + slack + PRUNE_T * LOG2E
        live = (cumend_ref[0, pl.ds(head, 1), :] * LOG2E <= reach) | (kidx >= n_full)
        first_live.append(jnp.min(jnp.where(live, kidx, nk).astype(F32)))
    ki_start = jnp.minimum(first_live[0], first_live[1]).astype(jnp.int32)

    npairs = lax.div(n_full - ki_start + 1, 2)

    def sweep(j, st):
        top = n_full - 1 - 2 * j
        st = consume(0, top, st)
        produce(0, top - 2)
        st = consume(1, top - 1, st)
        produce(1, top - 3)
        return st

    state = lax.fori_loop(0, npairs - 1, sweep, state)

    def last_pair(st):
        top = n_full + 1 - 2 * npairs
        st = consume(0, top, st)
        return lax.cond(top - 1 >= ki_start, lambda s: consume(1, top - 1, s), lambda s: s, st)

    state = lax.cond(npairs >= 1, last_pair, lambda st: st, state)

    o_t = jnp.concatenate([state[2] / state[1], state[5] / state[4]], axis=0)
    og_ref[0] = (o_t.T * ga_ref[0]).astype(BF16)


def _fox_prompt(qb, kb, vtb, cum, cumt, kn2, ga, *, tq):
    bsz, seq, adim = qb.shape
    nk, tk = vtb.shape[1], vtb.shape[3]
    assert seq % tq == 0 and tk % tq == 0 and nk * tk == seq
    nhp = adim // LANES
    cumend = cumt[:, :, tk - 1::tk]
    qtile = pl.BlockSpec((1, tq, LANES), lambda b, hp, qi: (b, qi, hp))
    return pl.pallas_call(
        functools.partial(_fox_prompt_kernel, tq=tq, tk=tk),
        grid=(bsz, nhp, seq // tq),
        in_specs=[qtile,
                  pl.BlockSpec((1, seq, LANES), lambda b, hp, qi: (b, 0, hp)),
                  pl.BlockSpec((1, nk, LANES, tk), lambda b, hp, qi: (b, 0, hp, 0)),
                  pl.BlockSpec((1, seq, N_HEADS), lambda b, hp, qi: (b, 0, 0)),
                  pl.BlockSpec((1, N_HEADS, tq), lambda b, hp, qi: (b, 0, qi)),
                  pl.BlockSpec((1, N_HEADS, nk), lambda b, hp, qi: (b, 0, 0)),
                  pl.BlockSpec((1, kn2.shape[1], N_HEADS, LANES), lambda b, hp, qi: (b, 0, 0, 0)),
                  qtile],
        out_specs=qtile,
        out_shape=jax.ShapeDtypeStruct((bsz, seq, adim), BF16),
        scratch_shapes=[pltpu.VMEM((HEADS_PER_BLOCK, seq, LANES), F32),
                        pltpu.VMEM((2, HEADS_PER_BLOCK, tk, tq), F32),
                        pltpu.VMEM((2, HEADS_PER_BLOCK, 1, tq), F32)],
        compiler_params=pltpu.CompilerParams(dimension_semantics=("arbitrary",) * 3, vmem_limit_bytes=VMEM_LIMIT),
        name="fox_prompt",
    )(qb, kb, vtb, cum, cumt, cumend, kn2, ga)


def _own_lanes(adim):
    sub = lax.broadcasted_iota(jnp.int32, (N_HEADS, adim), 0)
    lane_head = lax.broadcasted_iota(jnp.int32, (N_HEADS, adim), 1) // HEAD_DIM
    return sub == lane_head


def _fox_decode_scores_kernel(pt_ref, q_ref, kn_ref, lfn_ref, *rest, pages):
    k_refs = rest[:pages]
    lf_refs = rest[pages:2 * pages]
    pn_ref, pself_ref, live_ref, s_scr, m_scr, self_scr, carry_scr = rest[2 * pages:]
    del pt_ref
    j = pl.program_id(1)
    steps = pl.num_programs(1)
    adim = N_HEADS * HEAD_DIM

    qbd32 = jnp.where(_own_lanes(adim), jnp.broadcast_to(q_ref[0].astype(F32), (N_HEADS, adim)), 0.0)
    qbd = qbd32.astype(BF16)

    @pl.when(j == 0)
    def _():
        kn = kn_ref[0].astype(BF16).astype(F32)
        s_self = jnp.sum(qbd32 * kn, axis=1, keepdims=True)
        self_scr[...] = s_self
        m_scr[...] = s_self
        hs = lax.broadcasted_iota(jnp.int32, (N_HEADS, N_HEADS), 0)
        hl = lax.broadcasted_iota(jnp.int32, (N_HEADS, N_HEADS), 1)
        carry_scr[...] = jnp.sum(jnp.where(hs == hl, jnp.broadcast_to(lfn_ref[0], (N_HEADS, N_HEADS)), 0.0),
                                 axis=1, keepdims=True)

    lf_all = jnp.concatenate([r[...] for r in lf_refs], axis=0)
    jr = lax.broadcasted_iota(jnp.int32, (PAGE_SIZE, PAGE_SIZE), 0)
    sc = lax.broadcasted_iota(jnp.int32, (PAGE_SIZE, PAGE_SIZE), 1)
    later = jnp.where(jr > sc, 1.0, 0.0).astype(BF16)
    within = None
    for part in _split3(lf_all):
        w = _dot(part, later)
        within = w if within is None else within + w

    run = carry_scr[...]
    s_parts = [None] * pages
    for i in reversed(range(pages)):
        bias = within[i * N_HEADS:(i + 1) * N_HEADS, :] + run
        s_parts[i] = _dot(qbd, k_refs[i][...].astype(BF16)) + bias
        run = run + jnp.sum(lf_refs[i][...], axis=1, keepdims=True)
    carry_scr[...] = run

    s = jnp.concatenate(s_parts, axis=1)
    s_scr[steps - 1 - j] = s
    m_new = jnp.maximum(m_scr[...], jnp.max(s, axis=1, keepdims=True))
    m_scr[...] = m_new

    live_ref[0, 0] = jnp.max(s - m_new, axis=0, keepdims=True)

    @pl.when(j == steps - 1)
    def _():
        m = m_scr[...]
        p_self = jnp.exp(self_scr[...] - m)
        l = p_self
        for jj in range(s_scr.shape[0]):
            p = jnp.exp(s_scr[jj] - m)
            s_scr[jj] = p
            l = l + jnp.sum(p, axis=1, keepdims=True)
        inv = 1.0 / l
        for jj in range(s_scr.shape[0]):
            pn_ref[0, jj] = s_scr[jj] * inv
        pself_ref[0] = jnp.broadcast_to(p_self * inv, (N_HEADS, LANES))


def _fox_decode_pv_kernel(pt_ref, any_ref, pn_ref, pself_ref, vn_ref, ga_ref, *rest, pages, finalize):
    acc0_ref = rest[0]
    v_refs = rest[1:1 + pages]
    og_ref, acc_scr = rest[1 + pages:]
    del pt_ref
    b = pl.program_id(0)
    j = pl.program_id(1)
    adim = N_HEADS * HEAD_DIM

    @pl.when(j == 0)
    def _():
        acc_scr[...] = acc0_ref[0]

    @pl.when(any_ref[b, j] > 0)
    def _():
        pv = jnp.zeros((N_HEADS, adim), F32)
        per_group = pn_ref.shape[-1] // PAGE_SIZE
        for i in range(pages):
            g, c = divmod(i, per_group)
            pn = pn_ref[0, g, :, c * PAGE_SIZE:(c + 1) * PAGE_SIZE]
            pv = pv + _dot_nt(pn.astype(BF16), v_refs[i][...].astype(BF16))
        acc_scr[...] = acc_scr[...] + pv

    @pl.when(j == pl.num_programs(1) - 1)
    def _():
        if finalize:
            acc = acc_scr[...] + pself_ref[0][:, 0:1] * vn_ref[0].astype(BF16).astype(F32)
            o = jnp.sum(jnp.where(_own_lanes(adim), acc, 0.0), axis=0, keepdims=True)
            og_ref[0] = (o * ga_ref[0]).astype(BF16)
        else:
            og_ref[0] = acc_scr[...]


def _fox_decode(qb, k_new, v_new, lf_new, ga, cache_kt, cache_vt, cache_lft, page_table, layer, *, pages):
    db, adim = qb.shape
    n_pages = page_table.shape[1]
    assert n_pages % pages == 0 and pages <= LANES
    steps = n_pages // pages
    span = pages * PAGE_SIZE
    r3 = lambda a: a.reshape(db, 1, a.shape[-1])
    params = pltpu.CompilerParams(dimension_semantics=("arbitrary", "arbitrary"), vmem_limit_bytes=VMEM_LIMIT)

    row = lambda w: pl.BlockSpec((1, 1, w), lambda b, j, pt: (b, 0, 0))

    def page_spec(i, rows):
        return pl.BlockSpec((None, None, rows, PAGE_SIZE),
                            lambda b, j, pt: (layer, pt[b, (steps - 1 - j) * pages + i], 0, 0))

    pn, pself, live = pl.pallas_call(
        functools.partial(_fox_decode_scores_kernel, pages=pages),
        grid_spec=pltpu.PrefetchScalarGridSpec(
            num_scalar_prefetch=1, grid=(db, steps),
            in_specs=([row(adim), row(adim), row(N_HEADS)]
                      + [page_spec(i, adim) for i in range(pages)]
                      + [page_spec(i, N_HEADS) for i in range(pages)]),
            out_specs=(pl.BlockSpec((1, steps, N_HEADS, span), lambda b, j, pt: (b, 0, 0, 0)),
                       pl.BlockSpec((1, N_HEADS, LANES), lambda b, j, pt: (b, 0, 0)),
                       pl.BlockSpec((1, 1, 1, span), lambda b, j, pt: (b, steps - 1 - j, 0, 0))),
            scratch_shapes=[pltpu.VMEM((steps, N_HEADS, span), F32), pltpu.VMEM((N_HEADS, 1), F32),
                            pltpu.VMEM((N_HEADS, 1), F32), pltpu.VMEM((N_HEADS, 1), F32)]),
        out_shape=(jax.ShapeDtypeStruct((db, steps, N_HEADS, span), F32),
                   jax.ShapeDtypeStruct((db, N_HEADS, LANES), F32),
                   jax.ShapeDtypeStruct((db, steps, 1, span), F32)),
        compiler_params=params, name="fox_decode_scores",
    )(page_table, r3(qb), r3(k_new), r3(lf_new), *([cache_kt] * pages), *([cache_lft] * pages))

    group = DECODE_VALUE_GROUP if steps % DECODE_VALUE_GROUP == 0 else 1
    vsteps, vpages = steps // group, pages * group
    alive = (jnp.max(live.reshape(db, n_pages, PAGE_SIZE), axis=-1) >= DEAD_LOGIT).reshape(db, vsteps, vpages)
    table = page_table.reshape(db, vsteps, vpages)
    row2 = lambda w: pl.BlockSpec((1, 1, w), lambda b, j, pt, al: (b, 0, 0))
    acc_spec = pl.BlockSpec((1, N_HEADS, adim), lambda b, j, pt, al: (b, 0, 0))

    def value_pass(lo, hi, acc0, finalize):
        n = hi - lo
        sub = alive[:, lo:hi].reshape(db * n, vpages)
        order = jnp.arange(db * n, dtype=jnp.int32)[:, None]
        last = jnp.maximum(lax.cummax(jnp.where(sub, order, -1), axis=0), 0)
        page_ids = jnp.take_along_axis(table[:, lo:hi].reshape(db * n, vpages), last, axis=0).reshape(db, n * vpages)
        any_alive = jnp.any(sub, axis=1).reshape(db, n).astype(jnp.int32)

        def value_spec(i):
            return pl.BlockSpec((None, None, adim, PAGE_SIZE),
                                lambda b, j, pt, al: (layer, pt[b, j * vpages + i], 0, 0))

        return pl.pallas_call(
            functools.partial(_fox_decode_pv_kernel, pages=vpages, finalize=finalize),
            grid_spec=pltpu.PrefetchScalarGridSpec(
                num_scalar_prefetch=2, grid=(db, n),
                in_specs=([pl.BlockSpec((1, group, N_HEADS, span), lambda b, j, pt, al: (b, lo + j, 0, 0)),
                           pl.BlockSpec((1, N_HEADS, LANES), lambda b, j, pt, al: (b, 0, 0)),
                           row2(adim), row2(adim), acc_spec] + [value_spec(i) for i in range(vpages)]),
                out_specs=row2(adim) if finalize else acc_spec,
                scratch_shapes=[pltpu.VMEM((N_HEADS, adim), F32)]),
            out_shape=jax.ShapeDtypeStruct((db, 1, adim), BF16) if finalize
            else jax.ShapeDtypeStruct((db, N_HEADS, adim), F32),
            compiler_params=params, name="fox_decode_pv" if finalize else "fox_decode_pv_old",
        )(page_ids, any_alive, pn, pself, r3(v_new), r3(ga), acc0, *([cache_vt] * vpages))

    acc = jnp.zeros((db, N_HEADS, adim), F32)
    if vsteps > 1:
        acc = lax.cond(jnp.any(alive[:, :vsteps - 1]), lambda a: value_pass(0, vsteps - 1, a, False),
                       lambda a: a, acc)
    return value_pass(vsteps - 1, vsteps, acc, True).reshape(db, adim)


def _merge_out_kernel(x_ref, og_ref, yc_ref, g_ref, wmix_ref, wao_ref, wo_ref, fg_ref, out_ref, *, final):
    x = x_ref[0]
    d = x.shape[-1]
    h = _rmsnorm(x, g_ref[...]).astype(BF16)
    mix_c = _sigmoid(_dot(h, wmix_ref[:, 0:d]))
    mix_a = _sigmoid(_dot(h, wmix_ref[:, d:2 * d]))
    y_a = _dot(og_ref[0], wao_ref[...])
    m = mix_c * yc_ref[0] + mix_a * y_a
    y = x + _dot(m.astype(BF16), wo_ref[...])
    if final:
        y = _rmsnorm(y, fg_ref[...])
    out_ref[0] = y


def _merge_out(x, og, yc, layer, p, final_g, *, tq, final):
    bsz, seq, d = x.shape
    adim = og.shape[-1]
    assert seq % tq == 0
    tile = lambda w: pl.BlockSpec((1, tq, w), lambda b, t: (b, t, 0))
    in_specs = [tile(d), tile(adim), tile(d)] + [_const_spec(p[n].shape, layer) for n in
                                                  ("norm_g", "w_mix", "w_attn_out", "w_o")]
    in_specs.append(pl.BlockSpec(final_g.shape, lambda b, t: (0, 0)))
    return pl.pallas_call(
        functools.partial(_merge_out_kernel, final=final),
        grid=(bsz, seq // tq), in_specs=in_specs, out_specs=tile(d),
        out_shape=jax.ShapeDtypeStruct((bsz, seq, d), F32),
        compiler_params=pltpu.CompilerParams(dimension_semantics=("arbitrary", "arbitrary"),
                                             vmem_limit_bytes=VMEM_LIMIT),
        name="merge_out",
    )(x, og, yc, p["norm_g"], p["w_mix"], p["w_attn_out"], p["w_o"], final_g)


PROMPT_TILE = 512
ATTN_Q_TILE = 512
ATTN_K_CHUNK = 512
DECODE_PAGES = 16
DECODE_VALUE_GROUP = 2


def kernel(x_prompt, x_sample, cache_k, cache_v, cache_logf, state_conv, page_table, norm_g, w_in, b_f, w_dw,
           b_dw, ln_g, ln_b, w_conv_out, w_attn_out, w_o, final_g):
    depth, d = norm_g.shape
    bsz, seq, _ = x_prompt.shape
    db, dec_seq, _ = x_sample.shape
    assert dec_seq == 1
    cdim = w_dw.shape[-1]
    adim = N_HEADS * HEAD_DIM
    o_k = 3 * cdim + adim
    n_main = 3 * cdim + 4 * adim

    w_in_b = w_in.astype(BF16)
    w_f = w_in_b[:, :, n_main:n_main + N_HEADS]
    params = {
        "norm_g": norm_g.reshape(depth, 1, d),
        "w_main": w_in_b[:, :, :n_main],
        "w_kvt": jnp.swapaxes(w_in_b[:, :, o_k:o_k + 2 * adim], 1, 2),
        "w_f": jnp.pad(w_f, ((0, 0), (0, 0), (0, LANES - N_HEADS))),
        "w_ft": jnp.pad(jnp.swapaxes(w_f, 1, 2), ((0, 0), (0, BF16_ROWS - N_HEADS), (0, 0))),
        "w_mix": w_in_b[:, :, n_main + N_HEADS:],
        "b_f": jnp.pad(b_f, ((0, 0), (0, LANES - N_HEADS))).reshape(depth, 1, LANES),
        "b_ft": jnp.pad(b_f, ((0, 0), (0, BF16_ROWS - N_HEADS))).reshape(depth, BF16_ROWS, 1),
        "w_dw": w_dw,
        "b_dw": b_dw.reshape(depth, 1, cdim),
        "ln_g": ln_g.reshape(depth, 1, cdim),
        "ln_b": ln_b.reshape(depth, 1, cdim),
        "w_conv_out": w_conv_out.astype(BF16),
        "w_attn_out": w_attn_out.astype(BF16),
        "w_o": w_o.astype(BF16),
    }
    fg = final_g.reshape(1, d)
    n_pool = cache_k.shape[1]
    ckt = jnp.transpose(cache_k, (0, 1, 3, 4, 2)).reshape(depth, n_pool, adim, PAGE_SIZE)
    cvt = jnp.transpose(cache_v, (0, 1, 3, 4, 2)).reshape(depth, n_pool, adim, PAGE_SIZE)
    clft = jnp.transpose(cache_logf, (0, 1, 3, 2))
    state_t = jnp.transpose(state_conv, (0, 2, 1, 3))

    def heads_last(a_t):
        return jnp.transpose(a_t.reshape(depth, bsz, N_HEADS, HEAD_DIM, seq), (0, 1, 4, 2, 3))

    xp = x_prompt
    xs = x_sample.reshape(db, d)
    lp, cp, ksl, vsl, lsl, csl = ([] for _ in range(6))
    kv_stack = None
    for layer in range(depth):
        final = layer == depth - 1
        kt32, vt32, lft, qb, kb, vtb, ga, cum, cumt, yc, cst, kn2 = _proj_conv(xp, layer, params, kv_stack,
                                                                                tq=PROMPT_TILE, kc=ATTN_K_CHUNK)
        kv_stack = (kt32, vt32)
        og = _fox_prompt(qb, kb, vtb, cum, cumt, kn2, ga, tq=ATTN_Q_TILE)
        xp = _merge_out(xp, og, yc, layer, params, fg, tq=PROMPT_TILE, final=final)
        lp.append(jnp.transpose(lft, (0, 2, 1)))
        cp.append(cst)

        k_s, v_s, lf_s, q_s, ga_s, yc_s, u_s = _proj_conv_step(xs, state_t, layer, params)
        og_s = _fox_decode(q_s, k_s, v_s, lf_s, ga_s, ckt, cvt, clft, page_table, layer, pages=DECODE_PAGES)
        xs = _merge_out(xs[None], og_s[None], yc_s[None], layer, params, fg, tq=db, final=final)[0]
        ksl.append(k_s.reshape(db, 1, N_HEADS, HEAD_DIM))
        vsl.append(v_s.reshape(db, 1, N_HEADS, HEAD_DIM))
        lsl.append(lf_s.reshape(db, 1, N_HEADS))
        csl.append(jnp.transpose(jnp.concatenate([state_t[layer, 1:], u_s[None]], axis=0), (1, 0, 2)))

    return (xp, xs.reshape(db, 1, d), heads_last(kv_stack[0]), heads_last(kv_stack[1]), jnp.stack(lp), jnp.stack(cp),
            jnp.stack(ksl), jnp.stack(vsl), jnp.stack(lsl), jnp.stack(csl))
```
